```python
import jax, jax.numpy as jnp
from jax import lax
import numpy as np

D_MODEL = 2048
BATCH = 32
SEQ = 256
DEPTH = 4
DEC_BATCH = 4
DEC_SEQ = 4096
PAST_LEN = 512

GRID_W = 64
Q_BLOCK = 128
ROPE_THETA = 10000.0
NORM_EPS = 1e-6
N_MOD = 6
ATTN_HEADS = 4
ATTN_KV_HEADS = 2
HEAD_DIM = 128
LRU_WIDTH = 1024
LRU_BLOCKS = 8
LRU_BLOCK_W = LRU_WIDTH // LRU_BLOCKS
LRU_CONV_W = 4
LRU_C = 8.0
LRU_A_MIN = 0.9
LRU_A_MAX = 0.999
MLA_HEADS = 4
MLA_NOPE = 128
MLA_ROPE = 64
MLA_V = 128
MLA_KV_RANK = 512
ATTN_Q_COLS = ATTN_HEADS * HEAD_DIM
ATTN_KV_COLS = ATTN_KV_HEADS * HEAD_DIM
MLA_Q_COLS = MLA_HEADS * (MLA_NOPE + MLA_ROPE)
IN_SIZES = (ATTN_Q_COLS, ATTN_KV_COLS, ATTN_KV_COLS, LRU_WIDTH, LRU_WIDTH, MLA_Q_COLS, MLA_KV_RANK, MLA_ROPE)
D_IN = sum(IN_SIZES)
D_MIX = ATTN_HEADS * HEAD_DIM + LRU_WIDTH + MLA_HEADS * MLA_V
D_FF = 5632
FFN_CONV_W = 3

kernel_name = 'hybrid_gqa_rglru_mla_prefix_diffusion_step'


def rmsnorm(x, g):
    xf = x.astype(jnp.float32)
    xf = xf * lax.rsqrt(jnp.mean(xf * xf, axis=-1, keepdims=True) + NORM_EPS)
    return xf.astype(x.dtype) * g


def _rotate(x, pos):
    quarter = x.shape[-1] // 2
    inv_freq = ROPE_THETA ** (-jnp.arange(quarter, dtype=jnp.float32) / quarter)
    ang = pos.astype(jnp.float32)[:, None] * inv_freq[None, :]
    cos = jnp.cos(ang)[None, :, None, :].astype(x.dtype)
    sin = jnp.sin(ang)[None, :, None, :].astype(x.dtype)
    x1, x2 = x[..., :quarter], x[..., quarter:]
    return jnp.concatenate([x1 * cos - x2 * sin, x2 * cos + x1 * sin], axis=-1)


def axial_rope(x):
    n_tok = x.shape[1]
    rows = n_tok // GRID_W
    row = jnp.repeat(jnp.arange(rows, dtype=jnp.int32), GRID_W)
    col = jnp.tile(jnp.arange(GRID_W, dtype=jnp.int32), rows)
    half = x.shape[-1] // 2
    return jnp.concatenate([_rotate(x[..., :half], row), _rotate(x[..., half:], col)], axis=-1)


def attention(q, k, v):
    bsz, n_q, n_h, d_q = q.shape
    n_kv = k.shape[2]
    grp = n_h // n_kv
    d_v = v.shape[-1]
    scale = d_q ** -0.5
    n_blk = n_q // Q_BLOCK
    qb = q.reshape(bsz, n_blk, Q_BLOCK, n_kv, grp, d_q).transpose(1, 0, 2, 3, 4, 5)

    def one_block(q_blk):
        s = jnp.einsum('bqhgd,bkhd->bhgqk', q_blk, k, preferred_element_type=jnp.float32) * scale
        p = jax.nn.softmax(s, axis=-1).astype(v.dtype)
        return jnp.einsum('bhgqk,bkhd->bqhgd', p, v)

    o = lax.map(one_block, qb)
    return o.transpose(1, 0, 2, 3, 4, 5).reshape(bsz, n_q, n_h, d_v)


def dwconv(x, w, b, left):
    ksz = w.shape[0]
    n = x.shape[1]
    xp = jnp.pad(x, ((0, 0), (left, ksz - 1 - left), (0, 0)))
    y = xp[:, 0:n] * w[0]
    for j in range(1, ksz):
        y = y + xp[:, j:j + n] * w[j]
    return y + b


def _lin_combine(left, right):
    a_l, b_l = left
    a_r, b_r = right
    return a_l * a_r, a_r * b_l + b_r


def rglru(x, w_a, b_a, w_x, b_x, lam, h0, reverse):
    bsz, n, width = x.shape
    xb = x.reshape(bsz, n, LRU_BLOCKS, LRU_BLOCK_W)
    gate_r = jnp.einsum('bsnk,nkj->bsnj', xb, w_a).reshape(bsz, n, width) + b_a
    gate_i = jnp.einsum('bsnk,nkj->bsnj', xb, w_x).reshape(bsz, n, width) + b_x
    r = jax.nn.sigmoid(gate_r.astype(jnp.float32))
    i = jax.nn.sigmoid(gate_i.astype(jnp.float32))
    log_a = LRU_C * r * jax.nn.log_sigmoid(lam.astype(jnp.float32))
    a = jnp.exp(log_a)
    u = jnp.sqrt(-jnp.expm1(2.0 * log_a)) * (i * x.astype(jnp.float32))
    if reverse:
        a = jnp.flip(a, axis=1)
        u = jnp.flip(u, axis=1)
    u = u.at[:, 0].add(a[:, 0] * h0.astype(jnp.float32))
    _, h = lax.associative_scan(_lin_combine, (a, u), axis=1)
    if reverse:
        h = jnp.flip(h, axis=1)
    return h


def mixer(h, p, ctx):
    bsz, n, _ = h.shape
    offsets = np.cumsum(IN_SIZES)[:-1].tolist()
    proj = h @ p['w_in']
    q_a, k_a, v_a, x_r, g_r, q_m, ckv, k_r = jnp.split(proj, offsets, axis=-1)
    q_a = rmsnorm(q_a.reshape(bsz, n, ATTN_HEADS, HEAD_DIM), p['g_q'])
    k_a = rmsnorm(k_a.reshape(bsz, n, ATTN_KV_HEADS, HEAD_DIM), p['g_k'])
    v_a = v_a.reshape(bsz, n, ATTN_KV_HEADS, HEAD_DIM)
    q_m = q_m.reshape(bsz, n, MLA_HEADS, MLA_NOPE + MLA_ROPE)
    ckv = rmsnorm(ckv, p['g_kv'])
    x_r = dwconv(x_r, p['lru_conv_w'], p['lru_conv_b'], 1)
    if ctx is None:
        k_all, v_all, ckv_all, kr_all = k_a, v_a, ckv, k_r
        h0f = jnp.zeros((bsz, LRU_WIDTH), jnp.float32)
        h0b = jnp.zeros((bsz, LRU_WIDTH), jnp.float32)
    else:
        q_a = axial_rope(q_a)
        k_a = axial_rope(k_a)
        q_m = jnp.concatenate([q_m[..., :MLA_NOPE], axial_rope(q_m[..., MLA_NOPE:])], axis=-1)
        k_r = axial_rope(k_r[:, :, None, :])[:, :, 0, :]
        ctx_k, ctx_v, ctx_ckv, ctx_kr, ctx_state = ctx
        k_all = jnp.concatenate([ctx_k, k_a], axis=1)
        v_all = jnp.concatenate([ctx_v, v_a], axis=1)
        ckv_all = jnp.concatenate([ctx_ckv, ckv], axis=1)
        kr_all = jnp.concatenate([ctx_kr, k_r], axis=1)
        h0f = ctx_state[:, 0]
        h0b = ctx_state[:, 1]
    o_a = attention(q_a, k_all, v_all)
    hf = rglru(x_r, p['lru_w_a'][0], p['lru_b_a'][0], p['lru_w_x'][0], p['lru_b_x'][0], p['lru_lambda'][0], h0f, False)
    hb = rglru(x_r, p['lru_w_a'][1], p['lru_b_a'][1], p['lru_w_x'][1], p['lru_b_x'][1], p['lru_lambda'][1], h0b, True)
    y_r = (hf + hb).astype(h.dtype) * jax.nn.gelu(g_r)
    k_nope = jnp.einsum('bsr,rhd->bshd', ckv_all, p['w_uk'])
    v_m = jnp.einsum('bsr,rhd->bshd', ckv_all, p['w_uv'])
    k_m = jnp.concatenate([k_nope, jnp.broadcast_to(kr_all[:, :, None, :], k_nope.shape[:3] + (MLA_ROPE,))], axis=-1)
    o_m = attention(q_m, k_m, v_m)
    out = jnp.concatenate([o_a.reshape(bsz, n, -1), y_r, o_m.reshape(bsz, n, -1)], axis=-1) @ p['w_out']
    if ctx is None:
        final_state = jnp.stack([hf[:, -1], hb[:, 0]], axis=1).astype(h.dtype)
        return out, (k_a, v_a, ckv, k_r, final_state)
    return out, None


def conv_ffn(h, p):
    u = dwconv(h @ p['ffn_w_up'], p['ffn_conv_w'], p['ffn_conv_b'], 1)
    g, v = jnp.split(u, 2, axis=-1)
    return (jax.nn.silu(g) * v) @ p['ffn_w_down']


def layer(x, cond, p, ctx):
    mod = jax.nn.silu(cond) @ p['w_mod'] + p['b_mod']
    if mod.ndim == 2:
        mod = mod[:, None, :]
    sh_m, sc_m, gt_m, sh_f, sc_f, gt_f = jnp.split(mod, N_MOD, axis=-1)
    h = rmsnorm(x, p['g_pre_mix']) * (1.0 + sc_m) + sh_m
    m, st = mixer(h, p, ctx)
    x = x + gt_m * rmsnorm(m, p['g_post_mix'])
    h = rmsnorm(x, p['g_pre_ffn']) * (1.0 + sc_f) + sh_f
    x = x + gt_f * rmsnorm(conv_ffn(h, p), p['g_post_ffn'])
    return x, st


def setup_inputs(seed: int = 0) -> dict:
    key = jax.random.key(seed)
    keys = iter(jax.random.split(key, 48))
    f32 = jnp.float32

    def nrm(shape, scale):
        return scale * jax.random.normal(next(keys), shape, f32)

    def gain(shape):
        return 1.0 + nrm(shape, 0.05)

    a_init = jax.random.uniform(next(keys), (DEPTH, 2, LRU_WIDTH), f32, LRU_A_MIN, LRU_A_MAX)
    s_init = a_init ** (1.0 / LRU_C)
    lru_lambda = jnp.log(s_init) - jnp.log1p(-s_init)
    return {
        'x_prompt': nrm((BATCH, SEQ, D_MODEL), 1.0),
        'x_sample': nrm((DEC_BATCH, DEC_SEQ, D_MODEL), 1.0),
        'cache_attn_k': nrm((DEC_BATCH, DEPTH, PAST_LEN, ATTN_KV_HEADS, HEAD_DIM), 1.0),
        'cache_attn_v': nrm((DEC_BATCH, DEPTH, PAST_LEN, ATTN_KV_HEADS, HEAD_DIM), 1.0),
        'cache_mla_ckv': nrm((DEC_BATCH, DEPTH, PAST_LEN, MLA_KV_RANK), 1.0),
        'cache_mla_krope': nrm((DEC_BATCH, DEPTH, PAST_LEN, MLA_ROPE), 1.0),
        'state_lru': nrm((DEC_BATCH, DEPTH, 2, LRU_WIDTH), 0.5),
        'c': nrm((DEC_BATCH, D_MODEL), 1.0),
        'c_ctx': nrm((D_MODEL,), 1.0),
        'w_mod': nrm((DEPTH, D_MODEL, N_MOD * D_MODEL), 0.5 * D_MODEL ** -0.5),
        'b_mod': nrm((DEPTH, N_MOD * D_MODEL), 0.02),
        'g_pre_mix': gain((DEPTH, D_MODEL)),
        'g_post_mix': gain((DEPTH, D_MODEL)),
        'g_pre_ffn': gain((DEPTH, D_MODEL)),
        'g_post_ffn': gain((DEPTH, D_MODEL)),
        'w_in': nrm((DEPTH, D_MODEL, D_IN), D_MODEL ** -0.5),
        'g_q': gain((DEPTH, HEAD_DIM)),
        'g_k': gain((DEPTH, HEAD_DIM)),
        'lru_conv_w': nrm((DEPTH, LRU_CONV_W, LRU_WIDTH), LRU_CONV_W ** -0.5),
        'lru_conv_b': nrm((DEPTH, LRU_WIDTH), 0.02),
        'lru_w_a': nrm((DEPTH, 2, LRU_BLOCKS, LRU_BLOCK_W, LRU_BLOCK_W), LRU_BLOCK_W ** -0.5),
        'lru_b_a': nrm((DEPTH, 2, LRU_WIDTH), 0.02),
        'lru_w_x': nrm((DEPTH, 2, LRU_BLOCKS, LRU_BLOCK_W, LRU_BLOCK_W), LRU_BLOCK_W ** -0.5),
        'lru_b_x': nrm((DEPTH, 2, LRU_WIDTH), 0.02),
        'lru_lambda': lru_lambda,
        'g_kv': gain((DEPTH, MLA_KV_RANK)),
        'w_uk': nrm((DEPTH, MLA_KV_RANK, MLA_HEADS, MLA_NOPE), MLA_KV_RANK ** -0.5),
        'w_uv': nrm((DEPTH, MLA_KV_RANK, MLA_HEADS, MLA_V), MLA_KV_RANK ** -0.5),
        'w_out': nrm((DEPTH, D_MIX, D_MODEL), D_MIX ** -0.5),
        'ffn_w_up': nrm((DEPTH, D_MODEL, 2 * D_FF), D_MODEL ** -0.5),
        'ffn_conv_w': nrm((DEPTH, FFN_CONV_W, 2 * D_FF), FFN_CONV_W ** -0.5),
        'ffn_conv_b': nrm((DEPTH, 2 * D_FF), 0.02),
        'ffn_w_down': nrm((DEPTH, D_FF, D_MODEL), D_FF ** -0.5),
    }


def reference(x_prompt, x_sample, cache_attn_k, cache_attn_v, cache_mla_ckv, cache_mla_krope, state_lru,
              c, c_ctx, w_mod, b_mod, g_pre_mix, g_post_mix, g_pre_ffn, g_post_ffn, w_in, g_q, g_k,
              lru_conv_w, lru_conv_b, lru_w_a, lru_b_a, lru_w_x, lru_b_x, lru_lambda, g_kv, w_uk, w_uv,
              w_out, ffn_w_up, ffn_conv_w, ffn_conv_b, ffn_w_down):
    def layer_params(l):
        return {
            'w_mod': w_mod[l], 'b_mod': b_mod[l],
            'g_pre_mix': g_pre_mix[l], 'g_post_mix': g_post_mix[l],
            'g_pre_ffn': g_pre_ffn[l], 'g_post_ffn': g_post_ffn[l],
            'w_in': w_in[l], 'g_q': g_q[l], 'g_k': g_k[l],
            'lru_conv_w': lru_conv_w[l], 'lru_conv_b': lru_conv_b[l],
            'lru_w_a': lru_w_a[l], 'lru_b_a': lru_b_a[l],
            'lru_w_x': lru_w_x[l], 'lru_b_x': lru_b_x[l], 'lru_lambda': lru_lambda[l],
            'g_kv': g_kv[l], 'w_uk': w_uk[l], 'w_uv': w_uv[l], 'w_out': w_out[l],
            'ffn_w_up': ffn_w_up[l], 'ffn_conv_w': ffn_conv_w[l], 'ffn_conv_b': ffn_conv_b[l],
            'ffn_w_down': ffn_w_down[l],
        }

    xp = x_prompt
    ks, vs, ckvs, krs, sts = [], [], [], [], []
    for l in range(DEPTH):
        xp, (k_l, v_l, ckv_l, kr_l, st_l) = layer(xp, c_ctx, layer_params(l), None)
        ks.append(k_l)
        vs.append(v_l)
        ckvs.append(ckv_l)
        krs.append(kr_l)
        sts.append(st_l)

    xs = x_sample
    for l in range(DEPTH):
        ctx = (cache_attn_k[:, l], cache_attn_v[:, l], cache_mla_ckv[:, l], cache_mla_krope[:, l], state_lru[:, l])
        xs, _ = layer(xs, c, layer_params(l), ctx)

    new_attn_k = jnp.stack(ks, axis=1)
    new_attn_v = jnp.stack(vs, axis=1)
    new_mla_ckv = jnp.stack(ckvs, axis=1)
    new_mla_krope = jnp.stack(krs, axis=1)
    new_state_lru = jnp.stack(sts, axis=1)
    return (xp, xs, new_attn_k, new_attn_v, new_mla_ckv, new_mla_krope, new_state_lru)
```

```python
import functools

import numpy as np
import jax
import jax.numpy as jnp
from jax import lax
from jax.experimental import pallas as pl
from jax.experimental.pallas import tpu as pltpu

F32 = jnp.float32
BF16 = jnp.bfloat16

GRID_W = 64
ROPE_THETA = 10000.0
NORM_EPS = 1e-6
N_MOD = 6
ATTN_HEADS = 4
ATTN_KV_HEADS = 2
HEAD_DIM = 128
LRU_WIDTH = 1024
LRU_BLOCKS = 8
LRU_BLOCK_W = LRU_WIDTH // LRU_BLOCKS
LRU_CONV_W = 4
LRU_C = 8.0
MLA_HEADS = 4
MLA_NOPE = 128
MLA_ROPE = 64
MLA_V = 128
MLA_KV_RANK = 512
MLA_QPAD = 256

ATTN_Q_COLS = ATTN_HEADS * HEAD_DIM
ATTN_KV_COLS = ATTN_KV_HEADS * HEAD_DIM
MLA_Q_COLS = MLA_HEADS * (MLA_NOPE + MLA_ROPE)
OFF_QA = 0
OFF_KA = OFF_QA + ATTN_Q_COLS
OFF_VA = OFF_KA + ATTN_KV_COLS
OFF_XR = OFF_VA + ATTN_KV_COLS
OFF_GR = OFF_XR + LRU_WIDTH
OFF_QM = OFF_GR + LRU_WIDTH
OFF_CKV = OFF_QM + MLA_Q_COLS
OFF_KR = OFF_CKV + MLA_KV_RANK
D_IN = OFF_KR + MLA_ROPE
P_QA = 0
P_KA = P_QA + ATTN_Q_COLS
P_VA = P_KA + ATTN_KV_COLS
P_QM = P_VA + ATTN_KV_COLS
P_CKV = P_QM + MLA_HEADS * MLA_QPAD
P_KR = P_CKV + MLA_KV_RANK
P_END = P_KR + 128

VMEM_LIMIT_BYTES = 56 * 1024 * 1024


def _tile(n, pref, mult=128):
    if n <= pref:
        return n
    t = (pref // mult) * mult
    while t >= mult:
        if n % t == 0:
            return t
        t -= mult
    return n


def _params(*sem):
    return pltpu.CompilerParams(dimension_semantics=sem, vmem_limit_bytes=VMEM_LIMIT_BYTES)


def _rms(x):
    return x * lax.rsqrt(jnp.mean(x * x, axis=-1, keepdims=True) + NORM_EPS)


def _mod_kernel(c_ref, w_ref, b_ref, o_ref):
    c = c_ref[...]
    s = (c * jax.nn.sigmoid(c)).astype(BF16)
    o_ref[...] = jnp.dot(s, w_ref[...].astype(BF16), preferred_element_type=F32) + b_ref[...]


def _mod_all(cond, w_mod, b_mod):
    n_l, d, n = w_mod.shape
    tn = _tile(n, 1024)
    return pl.pallas_call(
        _mod_kernel,
        grid=(n_l, n // tn),
        in_specs=[
            pl.BlockSpec((8, d), lambda l, j: (0, 0)),
            pl.BlockSpec((None, d, tn), lambda l, j: (l, 0, j)),
            pl.BlockSpec((None, 1, tn), lambda l, j: (l, 0, j)),
        ],
        out_specs=pl.BlockSpec((None, 8, tn), lambda l, j: (l, 0, j)),
        out_shape=jax.ShapeDtypeStruct((n_l, 8, n), F32),
        compiler_params=_params("parallel", "parallel"),
        name="mod_vectors",
    )(cond, w_mod, b_mod.reshape(n_l, 1, n))


def _normmod_kernel(x_ref, mod_ref, g_ref, h_ref):
    m = mod_ref[...]
    h_ref[...] = (_rms(x_ref[...]) * g_ref[...] * (1.0 + m[1:2]) + m[0:1]).astype(BF16)


def _norm_mod(x, mod, g, l):
    m_rows, d = x.shape
    n_mod = mod.shape[1]
    tm = _tile(m_rows // n_mod, 512, 16)
    per = (m_rows // n_mod) // tm
    return pl.pallas_call(
        _normmod_kernel,
        grid=(m_rows // tm,),
        in_specs=[
            pl.BlockSpec((tm, d), lambda i: (i, 0)),
            pl.BlockSpec((None, None, 8, d), lambda i: (l, i // per, 0, 0)),
            pl.BlockSpec((None, 1, d), lambda i: (l, 0, 0)),
        ],
        out_specs=pl.BlockSpec((tm, d), lambda i: (i, 0)),
        out_shape=jax.ShapeDtypeStruct((m_rows, d), BF16),
        compiler_params=_params("parallel"),
        name="norm_mod",
    )(x, mod, g)


def _attn_proj_kernel(*refs, rope, emit_f32):
    h_ref, w_ref, gq_ref, gk_ref, gkv_ref = refs[:5]
    pos = 5
    if rope:
        c128_ref, s128_ref, c64_ref, s64_ref = refs[pos:pos + 4]
        pos += 4
    qa_ref, ka_ref, va_ref, qm_ref, ckv_ref, kr_ref = refs[pos:pos + 6]
    pos += 6
    if emit_f32:
        kf_ref, vf_ref, ckvf_ref, krf_ref = refs[pos:pos + 4]

    acc = jnp.dot(h_ref[...], w_ref[...], preferred_element_type=F32)
    tm = acc.shape[0]
    if rope:
        lane = lax.broadcasted_iota(jnp.int32, (tm, 128), 1)
        hi32 = (lane & 32) != 0
        hi16 = (lane & 16) != 0
        c128, s128 = c128_ref[...], s128_ref[...]
        c64, s64 = c64_ref[...], s64_ref[...]

        def rope128(x):
            return x * c128 + jnp.where(hi32, pltpu.roll(x, 32, 1), pltpu.roll(x, 96, 1)) * s128

        def rope64(x):
            return x * c64 + jnp.where(hi16, pltpu.roll(x, 16, 1), pltpu.roll(x, 112, 1)) * s64

    gq, gk = gq_ref[...], gk_ref[...]
    for hd in range(ATTN_HEADS):
        x = _rms(acc[:, P_QA + hd * HEAD_DIM:P_QA + (hd + 1) * HEAD_DIM]) * gq
        if rope:
            x = rope128(x)
        qa_ref[:, hd * HEAD_DIM:(hd + 1) * HEAD_DIM] = x.astype(BF16)
    for hd in range(ATTN_KV_HEADS):
        sl = slice(hd * HEAD_DIM, (hd + 1) * HEAD_DIM)
        x = _rms(acc[:, P_KA + hd * HEAD_DIM:P_KA + (hd + 1) * HEAD_DIM]) * gk
        if emit_f32:
            kf_ref[:, sl] = x
        if rope:
            x = rope128(x)
        ka_ref[:, sl] = x.astype(BF16)
    v = acc[:, P_VA:P_VA + ATTN_KV_COLS]
    va_ref[...] = v.astype(BF16)
    if emit_f32:
        vf_ref[...] = v
    for hd in range(MLA_HEADS):
        base = P_QM + hd * MLA_QPAD
        qm_ref[:, hd * MLA_QPAD:hd * MLA_QPAD + 128] = acc[:, base:base + 128].astype(BF16)
        x = acc[:, base + 128:base + 256]
        if rope:
            x = rope64(x)
        qm_ref[:, hd * MLA_QPAD + 128:(hd + 1) * MLA_QPAD] = x.astype(BF16)
    ckv = _rms(acc[:, P_CKV:P_CKV + MLA_KV_RANK]) * gkv_ref[...]
    ckv_ref[...] = ckv.astype(BF16)
    if emit_f32:
        ckvf_ref[...] = ckv
    x = acc[:, P_KR:P_KR + 128]
    if emit_f32:
        krf_ref[...] = x[:, :MLA_ROPE]
    if rope:
        x = rope64(x)
    kr_ref[...] = x.astype(BF16)


def _attn_proj(h, w_att, g_q, g_k, g_kv, l, seq, rope_tabs, emit_f32):
    m_rows, d = h.shape
    rope = rope_tabs is not None
    tm = _tile(seq, 512, 16)
    per = seq // tm
    row = lambda i: (i, 0)
    in_specs = [
        pl.BlockSpec((tm, d), row),
        pl.BlockSpec((None, d, P_END), lambda i: (l, 0, 0)),
        pl.BlockSpec((None, 1, HEAD_DIM), lambda i: (l, 0, 0)),
        pl.BlockSpec((None, 1, HEAD_DIM), lambda i: (l, 0, 0)),
        pl.BlockSpec((None, 1, MLA_KV_RANK), lambda i: (l, 0, 0)),
    ]
    args = [h, w_att, g_q, g_k, g_kv]
    if rope:
        in_specs += [pl.BlockSpec((tm, 128), lambda i: (i % per, 0))] * 4
        args += list(rope_tabs)
    widths = [ATTN_Q_COLS, ATTN_KV_COLS, ATTN_KV_COLS, MLA_HEADS * MLA_QPAD, MLA_KV_RANK, 128]
    out_specs = [pl.BlockSpec((tm, w), row) for w in widths]
    out_shape = [jax.ShapeDtypeStruct((m_rows, w), BF16) for w in widths]
    if emit_f32:
        fw = [ATTN_KV_COLS, ATTN_KV_COLS, MLA_KV_RANK, MLA_ROPE]
        out_specs += [pl.BlockSpec((tm, w), row) for w in fw]
        out_shape += [jax.ShapeDtypeStruct((m_rows, w), F32) for w in fw]
    return pl.pallas_call(
        functools.partial(_attn_proj_kernel, rope=rope, emit_f32=emit_f32),
        grid=(m_rows // tm,),
        in_specs=in_specs,
        out_specs=out_specs,
        out_shape=out_shape,
        compiler_params=_params("parallel"),
        name="attn_proj",
    )(*args)


def _lru_proj_kernel(h_ref, w_ref, x_ref, g_ref):
    acc = jnp.dot(h_ref[...], w_ref[...], preferred_element_type=F32)
    x_ref[...] = acc[:, :LRU_WIDTH]
    g_ref[...] = acc[:, LRU_WIDTH:]


def _lru_proj(h, w_lru, l):
    m_rows, d = h.shape
    tm = _tile(m_rows, 512, 16)
    return pl.pallas_call(
        _lru_proj_kernel,
        grid=(m_rows // tm,),
        in_specs=[
            pl.BlockSpec((tm, d), lambda i: (i, 0)),
            pl.BlockSpec((None, d, 2 * LRU_WIDTH), lambda i: (l, 0, 0)),
        ],
        out_specs=[pl.BlockSpec((tm, LRU_WIDTH), lambda i: (i, 0))] * 2,
        out_shape=[jax.ShapeDtypeStruct((m_rows, LRU_WIDTH), F32)] * 2,
        compiler_params=_params("parallel"),
        name="lru_proj",
    )(h, w_lru)


def _attn_kernel(q_ref, k_ref, v_ref, o_ref, *, n_kv, group, dq, dv, tk, scale):
    tq = q_ref.shape[0]
    n_chunks = k_ref.shape[0] // tk
    rows = group * tq
    for hk in range(n_kv):
        qs = [q_ref[:, (hk * group + g) * dq:(hk * group + g + 1) * dq] for g in range(group)]
        q = qs[0] if group == 1 else jnp.concatenate(qs, axis=0)

        def body(c, carry, q=q, hk=hk):
            m, l, acc = carry
            off = pl.multiple_of(c * tk, tk)
            kc = k_ref[pl.ds(off, tk), hk * dq:(hk + 1) * dq]
            vc = v_ref[pl.ds(off, tk), hk * dv:(hk + 1) * dv]
            s = lax.dot_general(q, kc, (((1,), (1,)), ((), ())), preferred_element_type=F32) * scale
            m_new = jnp.maximum(m, jnp.max(s, axis=-1, keepdims=True))
            p = jnp.exp(s - m_new)
            alpha = jnp.exp(m - m_new)
            l = alpha * l + jnp.sum(p, axis=-1, keepdims=True)
            acc = alpha * acc + jnp.dot(p.astype(BF16), vc, preferred_element_type=F32)
            return m_new, l, acc

        init = (jnp.full((rows, 1), -jnp.inf, F32), jnp.zeros((rows, 1), F32), jnp.zeros((rows, dv), F32))
        _, l, acc = lax.fori_loop(0, n_chunks, body, init)
        o = acc / l
        for g in range(group):
            o_ref[:, (hk * group + g) * dv:(hk * group + g + 1) * dv] = o[g * tq:(g + 1) * tq].astype(BF16)


def _attention(q, k, v, n_kv, group, dq, dv, scale):
    bsz, n_q, qw = q.shape
    n_k = k.shape[1]
    tq = _tile(n_q, 256, 16)
    tk = _tile(n_k, 512, 128)
    return pl.pallas_call(
        functools.partial(_attn_kernel, n_kv=n_kv, group=group, dq=dq, dv=dv, tk=tk, scale=scale),
        grid=(bsz, n_q // tq),
        in_specs=[
            pl.BlockSpec((None, tq, qw), lambda b, i: (b, i, 0)),
            pl.BlockSpec((None, n_k, k.shape[2]), lambda b, i: (b, 0, 0)),
            pl.BlockSpec((None, n_k, v.shape[2]), lambda b, i: (b, 0, 0)),
        ],
        out_specs=pl.BlockSpec((None, tq, n_kv * group * dv), lambda b, i: (b, i, 0)),
        out_shape=jax.ShapeDtypeStruct((bsz, n_q, n_kv * group * dv), BF16),
        compiler_params=_params("parallel", "parallel"),
        name="attention",
    )(q, k, v)


def _kv_up_kernel(ckv_ref, kr_ref, w_ref, k_ref, v_ref):
    r = jnp.dot(ckv_ref[...], w_ref[...], preferred_element_type=F32)
    kr = kr_ref[...]
    for hd in range(MLA_HEADS):
        k_ref[:, hd * MLA_QPAD:hd * MLA_QPAD + MLA_NOPE] = r[:, hd * MLA_NOPE:(hd + 1) * MLA_NOPE].astype(BF16)
        k_ref[:, hd * MLA_QPAD + MLA_NOPE:(hd + 1) * MLA_QPAD] = kr
    v_ref[...] = r[:, MLA_HEADS * MLA_NOPE:].astype(BF16)


def _kv_up(ckv, kr, w_ukv, l):
    m_rows = ckv.shape[0]
    tm = _tile(m_rows, 512, 16)
    n_out = MLA_HEADS * (MLA_NOPE + MLA_V)
    return pl.pallas_call(
        _kv_up_kernel,
        grid=(m_rows // tm,),
        in_specs=[
            pl.BlockSpec((tm, MLA_KV_RANK), lambda i: (i, 0)),
            pl.BlockSpec((tm, 128), lambda i: (i, 0)),
            pl.BlockSpec((None, MLA_KV_RANK, n_out), lambda i: (l, 0, 0)),
        ],
        out_specs=[
            pl.BlockSpec((tm, MLA_HEADS * MLA_QPAD), lambda i: (i, 0)),
            pl.BlockSpec((tm, MLA_HEADS * MLA_V), lambda i: (i, 0)),
        ],
        out_shape=[
            jax.ShapeDtypeStruct((m_rows, MLA_HEADS * MLA_QPAD), BF16),
            jax.ShapeDtypeStruct((m_rows, MLA_HEADS * MLA_V), BF16),
        ],
        compiler_params=_params("parallel"),
        name="mla_kv_up",
    )(ckv, kr, w_ukv)


def _lru_kernel(*refs, reverse, combine, n_chunks):
    xp_ref, xm_ref, xn_ref, cw_ref, cb_ref, wax_ref, ba_ref, bx_ref, lam_ref, h0_ref = refs[:10]
    if combine:
        hb_ref, g_ref, y_ref, st_ref, a_s, u_s, h_s, carry = refs[10:]
    else:
        ho_ref, st_ref, a_s, u_s, h_s, carry = refs[10:]
    c = pl.program_id(1)
    cc = (n_chunks - 1 - c) if reverse else c
    tc = xm_ref.shape[0]

    xp = jnp.where(cc > 0, xp_ref[...], 0.0)
    xn = jnp.where(cc < n_chunks - 1, xn_ref[...], 0.0)
    xe = jnp.concatenate([xp, xm_ref[...], xn], axis=0)
    cw = cw_ref[...]
    xc = xe[7:7 + tc] * cw[0:1]
    for j in range(1, LRU_CONV_W):
        xc = xc + xe[7 + j:7 + j + tc] * cw[j:j + 1]
    xc = xc + cb_ref[...]

    lam = lam_ref[...]
    ls_c = LRU_C * (jnp.minimum(lam, 0.0) - jnp.log1p(jnp.exp(-jnp.abs(lam))))
    xcb = xc.astype(BF16)
    for n in range(LRU_BLOCKS):
        sl = slice(n * LRU_BLOCK_W, (n + 1) * LRU_BLOCK_W)
        gg = jnp.dot(xcb[:, sl], wax_ref[n], preferred_element_type=F32)
        r = jax.nn.sigmoid(gg[:, :LRU_BLOCK_W] + ba_ref[:, sl])
        i = jax.nn.sigmoid(gg[:, LRU_BLOCK_W:] + bx_ref[:, sl])
        log_a = r * ls_c[:, sl]
        a = jnp.exp(log_a)
        a_s[:, sl] = a
        u_s[:, sl] = jnp.sqrt(-jnp.tanh(log_a) * (a * a + 1.0)) * (i * xc[:, sl])

    @pl.when(c == 0)
    def _():
        carry[...] = h0_ref[...]

    def body(j, h):
        t = (tc - 1 - j) if reverse else j
        h = a_s[pl.ds(t, 1), :] * h + u_s[pl.ds(t, 1), :]
        h_s[pl.ds(t, 1), :] = h
        return h

    h = lax.fori_loop(0, tc, body, carry[...], unroll=8)
    carry[...] = h
    st_ref[...] = h
    if combine:
        y_ref[...] = ((h_s[...] + hb_ref[...]) * jax.nn.gelu(g_ref[...])).astype(BF16)
    else:
        ho_ref[...] = h_s[...]


def _lru_pass(x_r, conv_w, conv_b, wax, b_a, b_x, lam, h0, l, direction, hb=None, g_r=None):
    bsz, seq, w = x_r.shape
    combine = hb is not None
    reverse = direction == 1
    tc = _tile(seq, 256, 8)
    n_chunks = seq // tc
    t8 = tc // 8
    n8 = seq // 8
    d = direction

    def chunk(c):
        return (n_chunks - 1 - c) if reverse else c

    main = lambda b, c: (b, chunk(c), 0)
    in_specs = [
        pl.BlockSpec((None, 8, w), lambda b, c: (b, jnp.maximum(chunk(c) * t8 - 1, 0), 0)),
        pl.BlockSpec((None, tc, w), main),
        pl.BlockSpec((None, 8, w), lambda b, c: (b, jnp.minimum((chunk(c) + 1) * t8, n8 - 1), 0)),
        pl.BlockSpec((None, LRU_CONV_W, w), lambda b, c: (l, 0, 0)),
        pl.BlockSpec((None, 1, w), lambda b, c: (l, 0, 0)),
        pl.BlockSpec((None, None, LRU_BLOCKS, LRU_BLOCK_W, 2 * LRU_BLOCK_W), lambda b, c: (l, d, 0, 0, 0)),
        pl.BlockSpec((None, None, 1, w), lambda b, c: (l, d, 0, 0)),
        pl.BlockSpec((None, None, 1, w), lambda b, c: (l, d, 0, 0)),
        pl.BlockSpec((None, None, 1, w), lambda b, c: (l, d, 0, 0)),
        pl.BlockSpec((None, None, 1, w), lambda b, c: (b, d, 0, 0)),
    ]
    args = [x_r, x_r, x_r, conv_w, conv_b, wax, b_a, b_x, lam, h0]
    st_spec = pl.BlockSpec((None, 1, w), lambda b, c: (b, 0, 0))
    st_shape = jax.ShapeDtypeStruct((bsz, 1, w), F32)
    if combine:
        in_specs += [pl.BlockSpec((None, tc, w), main), pl.BlockSpec((None, tc, w), main)]
        args += [hb, g_r]
        out_specs = [pl.BlockSpec((None, tc, w), main), st_spec]
        out_shape = [jax.ShapeDtypeStruct((bsz, seq, w), BF16), st_shape]
    else:
        out_specs = [pl.BlockSpec((None, tc, w), main), st_spec]
        out_shape = [jax.ShapeDtypeStruct((bsz, seq, w), F32), st_shape]
    return pl.pallas_call(
        functools.partial(_lru_kernel, reverse=reverse, combine=combine, n_chunks=n_chunks),
        grid=(bsz, n_chunks),
        in_specs=in_specs,
        out_specs=out_specs,
        out_shape=out_shape,
        scratch_shapes=[pltpu.VMEM((tc, w), F32)] * 3 + [pltpu.VMEM((1, w), F32)],
        compiler_params=_params("parallel", "arbitrary"),
        name="rglru_bwd" if reverse else "rglru_fwd",
    )(*args)


def _mix_out_kernel(oa_ref, yr_ref, om_ref, w_ref, x_ref, mod_ref, gpost_ref, gffn_ref, xo_ref, h_ref):
    n_a = oa_ref.shape[1]
    n_r = yr_ref.shape[1]
    out = jnp.dot(oa_ref[...], w_ref[0:n_a, :], preferred_element_type=F32)
    out = out + jnp.dot(yr_ref[...], w_ref[n_a:n_a + n_r, :], preferred_element_type=F32)
    out = out + jnp.dot(om_ref[...], w_ref[n_a + n_r:, :], preferred_element_type=F32)
    m = mod_ref[...]
    xn = x_ref[...] + m[2:3] * (_rms(out) * gpost_ref[...])
    xo_ref[...] = xn
    h_ref[...] = (_rms(xn) * gffn_ref[...] * (1.0 + m[4:5]) + m[3:4]).astype(BF16)


def _mix_out(o_a, y_r, o_m, w_out, x, mod, g_post, g_ffn, l):
    m_rows, d = x.shape
    n_mod = mod.shape[1]
    tm = _tile(m_rows // n_mod, 512, 16)
    per = (m_rows // n_mod) // tm
    row = lambda i: (i, 0)
    d_mix = w_out.shape[1]
    return pl.pallas_call(
        _mix_out_kernel,
        grid=(m_rows // tm,),
        in_specs=[
            pl.BlockSpec((tm, o_a.shape[1]), row),
            pl.BlockSpec((tm, y_r.shape[1]), row),
            pl.BlockSpec((tm, o_m.shape[1]), row),
            pl.BlockSpec((None, d_mix, d), lambda i: (l, 0, 0)),
            pl.BlockSpec((tm, d), row),
            pl.BlockSpec((None, None, 8, d), lambda i: (l, i // per, 0, 0)),
            pl.BlockSpec((None, 1, d), lambda i: (l, 0, 0)),
            pl.BlockSpec((None, 1, d), lambda i: (l, 0, 0)),
        ],
        out_specs=[pl.BlockSpec((tm, d), row), pl.BlockSpec((tm, d), row)],
        out_shape=[jax.ShapeDtypeStruct((m_rows, d), F32), jax.ShapeDtypeStruct((m_rows, d), BF16)],
        compiler_params=_params("parallel"),
        name="mix_out",
    )(o_a, y_r, o_m, w_out, x, mod, g_post, g_ffn)


HALO = 16


def _ffn_up_kernel(hp_ref, hm_ref, hn_ref, wg_ref, wv_ref, cwg_ref, cwv_ref, cbg_ref, cbv_ref, o_ref, he_ref, *, seq):
    i = pl.program_id(0)
    tm = hm_ref.shape[0]

    @pl.when(pl.program_id(1) == 0)
    def _():
        he_ref[0:HALO, :] = hp_ref[...]
        he_ref[HALO:HALO + tm, :] = hm_ref[...]
        he_ref[HALO + tm:, :] = hn_ref[...]

    he = he_ref[...]
    row = i * tm + lax.broadcasted_iota(jnp.int32, (tm, 1), 0)
    pos = row % seq
    has_prev = pos != 0
    has_next = pos != seq - 1

    def branch(w_ref, cw_ref, cb_ref):
        u = jnp.dot(he, w_ref[...], preferred_element_type=F32)
        cw = cw_ref[...]
        up = jnp.where(has_prev, u[HALO - 1:HALO - 1 + tm], 0.0)
        un = jnp.where(has_next, u[HALO + 1:HALO + 1 + tm], 0.0)
        return up * cw[0:1] + u[HALO:HALO + tm] * cw[1:2] + un * cw[2:3] + cb_ref[...]

    g = branch(wg_ref, cwg_ref, cbg_ref)
    v = branch(wv_ref, cwv_ref, cbv_ref)
    o_ref[...] = (g * jax.nn.sigmoid(g) * v).astype(BF16)


def _ffn_up(h, w_up, conv_w, conv_b, l, seq):
    m_rows, d = h.shape
    d_ff = w_up.shape[2] // 2
    tm = _tile(m_rows, 1024, 16)
    if tm % seq != 0 and seq % tm != 0:
        tm = _tile(seq, 1024, 16)
    tn = _tile(d_ff, 512)
    nj = d_ff // tn
    th = tm // HALO
    nh = m_rows // HALO
    return pl.pallas_call(
        functools.partial(_ffn_up_kernel, seq=seq),
        grid=(m_rows // tm, nj),
        in_specs=[
            pl.BlockSpec((HALO, d), lambda i, j: (jnp.maximum(i * th - 1, 0), 0)),
            pl.BlockSpec((tm, d), lambda i, j: (i, 0)),
            pl.BlockSpec((HALO, d), lambda i, j: (jnp.minimum((i + 1) * th, nh - 1), 0)),
            pl.BlockSpec((None, d, tn), lambda i, j: (l, 0, j)),
            pl.BlockSpec((None, d, tn), lambda i, j: (l, 0, j + nj)),
            pl.BlockSpec((None, 3, tn), lambda i, j: (l, 0, j)),
            pl.BlockSpec((None, 3, tn), lambda i, j: (l, 0, j + nj)),
            pl.BlockSpec((None, 1, tn), lambda i, j: (l, 0, j)),
            pl.BlockSpec((None, 1, tn), lambda i, j: (l, 0, j + nj)),
        ],
        out_specs=pl.BlockSpec((tm, tn), lambda i, j: (i, j)),
        out_shape=jax.ShapeDtypeStruct((m_rows, d_ff), BF16),
        scratch_shapes=[pltpu.VMEM((tm + 2 * HALO, d), BF16)],
        compiler_params=_params("parallel", "arbitrary"),
        name="ffn_up",
    )(h, h, h, w_up, w_up, conv_w, conv_w, conv_b, conv_b)


def _ffn_down_kernel(*refs, has_next):
    a_ref, w_ref, x_ref, mod_ref, gpost_ref = refs[:5]
    if has_next:
        modn_ref, gnext_ref, xo_ref, hn_ref, acc_ref = refs[5:]
    else:
        xo_ref, acc_ref = refs[5:]
    k = pl.program_id(1)

    @pl.when(k == 0)
    def _():
        acc_ref[...] = jnp.zeros_like(acc_ref)

    acc_ref[...] += jnp.dot(a_ref[...], w_ref[...], preferred_element_type=F32)

    @pl.when(k == pl.num_programs(1) - 1)
    def _():
        m = mod_ref[...]
        xn = x_ref[...] + m[5:6] * (_rms(acc_ref[...]) * gpost_ref[...])
        xo_ref[...] = xn
        if has_next:
            mn = modn_ref[...]
            hn_ref[...] = (_rms(xn) * gnext_ref[...] * (1.0 + mn[1:2]) + mn[0:1]).astype(BF16)


def _ffn_down(act, w_down, x, mod, g_post, g_pre_mix, l, has_next):
    m_rows, d = x.shape
    d_ff = act.shape[1]
    n_mod = mod.shape[1]
    tm = _tile(m_rows // n_mod, 512, 16)
    per = (m_rows // n_mod) // tm
    tk = _tile(d_ff, 1408)
    row = lambda i, k: (i, 0)
    in_specs = [
        pl.BlockSpec((tm, tk), lambda i, k: (i, k)),
        pl.BlockSpec((None, tk, d), lambda i, k: (l, k, 0)),
        pl.BlockSpec((tm, d), row),
        pl.BlockSpec((None, None, 8, d), lambda i, k: (l, i // per, 0, 0)),
        pl.BlockSpec((None, 1, d), lambda i, k: (l, 0, 0)),
    ]
    args = [act, w_down, x, mod, g_post]
    out_specs = [pl.BlockSpec((tm, d), row)]
    out_shape = [jax.ShapeDtypeStruct((m_rows, d), F32)]
    if has_next:
        in_specs += [
            pl.BlockSpec((None, None, 8, d), lambda i, k: (l + 1, i // per, 0, 0)),
            pl.BlockSpec((None, 1, d), lambda i, k: (l + 1, 0, 0)),
        ]
        args += [mod, g_pre_mix]
        out_specs.append(pl.BlockSpec((tm, d), row))
        out_shape.append(jax.ShapeDtypeStruct((m_rows, d), BF16))
    res = pl.pallas_call(
        functools.partial(_ffn_down_kernel, has_next=has_next),
        grid=(m_rows // tm, d_ff // tk),
        in_specs=in_specs,
        out_specs=out_specs,
        out_shape=out_shape,
        scratch_shapes=[pltpu.VMEM((tm, d), F32)],
        compiler_params=_params("parallel", "arbitrary"),
        name="ffn_down",
    )(*args)
    return (res[0], res[1]) if has_next else (res[0], None)


def _rope_tables(n_tok):
    rows = n_tok // GRID_W
    row = jnp.repeat(jnp.arange(rows, dtype=jnp.int32), GRID_W)
    col = jnp.tile(jnp.arange(GRID_W, dtype=jnp.int32), rows)

    def tabs(rot_dim):
        quarter = rot_dim // 4
        inv_freq = ROPE_THETA ** (-jnp.arange(quarter, dtype=F32) / quarter)
        cs, ss = [], []
        for p in (row, col):
            ang = p.astype(F32)[:, None] * inv_freq[None, :]
            cs += [jnp.cos(ang), jnp.cos(ang)]
            ss += [-jnp.sin(ang), jnp.sin(ang)]
        return jnp.concatenate(cs, axis=-1), jnp.concatenate(ss, axis=-1)

    c128, s128 = tabs(HEAD_DIM)
    c64, s64 = tabs(MLA_ROPE)
    pad = ((0, 0), (0, 128 - MLA_ROPE))
    return c128, s128, jnp.pad(c64, pad), jnp.pad(s64, pad)


def _run_path(x, bsz, seq, mod, wts, ctx):
    n_l = wts["w_att"].shape[0]
    d = x.shape[1]
    latent = ctx is not None
    rope_tabs = _rope_tables(seq) if latent else None
    outs = {"k": [], "v": [], "ckv": [], "kr": [], "st": []}
    h = _norm_mod(x, mod, wts["g_pre_mix"], 0)
    for l in range(n_l):
        proj = _attn_proj(h, wts["w_att"], wts["g_q"], wts["g_k"], wts["g_kv"], l, seq, rope_tabs, not latent)
        q_a, k_a, v_a, q_m, ckv, k_r = proj[:6]
        x_r, g_r = _lru_proj(h, wts["w_lru"], l)

        k_all = k_a.reshape(bsz, seq, ATTN_KV_COLS)
        v_all = v_a.reshape(bsz, seq, ATTN_KV_COLS)
        ckv_all = ckv.reshape(bsz, seq, MLA_KV_RANK)
        kr_all = k_r.reshape(bsz, seq, 128)
        if latent:
            k_all = jnp.concatenate([ctx["k"][:, l], k_all], axis=1)
            v_all = jnp.concatenate([ctx["v"][:, l], v_all], axis=1)
            ckv_all = jnp.concatenate([ctx["ckv"][:, l], ckv_all], axis=1)
            kr_all = jnp.concatenate([ctx["kr"][:, l], kr_all], axis=1)
            h0 = ctx["st"]
            h0_l = l
        else:
            kf, vf, ckvf, krf = proj[6:]
            outs["k"].append(kf.reshape(bsz, seq, ATTN_KV_HEADS, HEAD_DIM))
            outs["v"].append(vf.reshape(bsz, seq, ATTN_KV_HEADS, HEAD_DIM))
            outs["ckv"].append(ckvf.reshape(bsz, seq, MLA_KV_RANK))
            outs["kr"].append(krf.reshape(bsz, seq, MLA_ROPE))
            h0 = jnp.zeros((bsz, 1, 2, 1, LRU_WIDTH), F32)
            h0_l = 0
        n_k = k_all.shape[1]

        o_a = _attention(q_a.reshape(bsz, seq, ATTN_Q_COLS), k_all, v_all,
                         ATTN_KV_HEADS, ATTN_HEADS // ATTN_KV_HEADS, HEAD_DIM, HEAD_DIM, HEAD_DIM ** -0.5)
        k_m, v_m = _kv_up(ckv_all.reshape(bsz * n_k, MLA_KV_RANK), kr_all.reshape(bsz * n_k, 128), wts["w_ukv"], l)
        o_m = _attention(q_m.reshape(bsz, seq, MLA_HEADS * MLA_QPAD),
                         k_m.reshape(bsz, n_k, MLA_HEADS * MLA_QPAD), v_m.reshape(bsz, n_k, MLA_HEADS * MLA_V),
                         MLA_HEADS, 1, MLA_QPAD, MLA_V, (MLA_NOPE + MLA_ROPE) ** -0.5)

        x_r3 = x_r.reshape(bsz, seq, LRU_WIDTH)
        lru_args = (x_r3, wts["lru_conv_w"], wts["lru_conv_b"], wts["lru_wax"], wts["lru_b_a"], wts["lru_b_x"],
                    wts["lru_lambda"], h0[:, h0_l])
        hb, st_b = _lru_pass(*lru_args, l, 1)
        y_r, st_f = _lru_pass(*lru_args, l, 0, hb=hb, g_r=g_r.reshape(bsz, seq, LRU_WIDTH))
        if not latent:
            outs["st"].append(jnp.concatenate([st_f, st_b], axis=1))

        x, h2 = _mix_out(o_a.reshape(bsz * seq, -1), y_r.reshape(bsz * seq, LRU_WIDTH), o_m.reshape(bsz * seq, -1),
                         wts["w_out"], x, mod, wts["g_post_mix"], wts["g_pre_ffn"], l)
        act = _ffn_up(h2, wts["ffn_w_up"], wts["ffn_conv_w"], wts["ffn_conv_b"], l, seq)
        x, h = _ffn_down(act, wts["ffn_w_down"], x, mod, wts["g_post_ffn"], wts["g_pre_mix"], l, l + 1 < n_l)
    return x.reshape(bsz, seq, d), outs


def kernel(x_prompt, x_sample, cache_attn_k, cache_attn_v, cache_mla_ckv, cache_mla_krope, state_lru, c, c_ctx, w_mod, b_mod, g_pre_mix, g_post_mix, g_pre_ffn, g_post_ffn, w_in, g_q, g_k, lru_conv_w, lru_conv_b, lru_w_a, lru_b_a, lru_w_x, lru_b_x, lru_lambda, g_kv, w_uk, w_uv, w_out, ffn_w_up, ffn_conv_w, ffn_conv_b, ffn_w_down):
    n_l, d, _ = w_in.shape
    bsz, seq, _ = x_prompt.shape
    dbsz, dseq, _ = x_sample.shape
    assert dbsz <= 7 and w_in.shape[2] == D_IN

    cond = jnp.zeros((8, d), F32).at[0].set(c_ctx).at[1:1 + dbsz].set(c)
    mod_all = _mod_all(cond, w_mod, b_mod).reshape(n_l, 8, N_MOD, d)
    mod_all = jnp.pad(mod_all, ((0, 0), (0, 0), (0, 8 - N_MOD), (0, 0)))
    mod_p = mod_all[:, 0:1]
    mod_s = mod_all[:, 1:1 + dbsz]

    qm = w_in[:, :, OFF_QM:OFF_CKV].reshape(n_l, d, MLA_HEADS, MLA_NOPE + MLA_ROPE)
    qm = jnp.pad(qm, ((0, 0), (0, 0), (0, 0), (0, MLA_QPAD - MLA_NOPE - MLA_ROPE))).reshape(n_l, d, MLA_HEADS * MLA_QPAD)
    w_att = jnp.concatenate(
        [w_in[:, :, OFF_QA:OFF_XR], qm, w_in[:, :, OFF_CKV:OFF_KR],
         jnp.pad(w_in[:, :, OFF_KR:], ((0, 0), (0, 0), (0, 128 - MLA_ROPE)))], axis=-1).astype(BF16)
    vec = lambda a: a.reshape(a.shape[:-1] + (1, a.shape[-1]))
    wts = {
        "w_att": w_att,
        "w_lru": w_in[:, :, OFF_XR:OFF_QM].astype(BF16),
        "g_q": vec(g_q), "g_k": vec(g_k), "g_kv": vec(g_kv),
        "g_pre_mix": vec(g_pre_mix), "g_post_mix": vec(g_post_mix),
        "g_pre_ffn": vec(g_pre_ffn), "g_post_ffn": vec(g_post_ffn),
        "lru_conv_w": lru_conv_w, "lru_conv_b": vec(lru_conv_b),
        "lru_wax": jnp.concatenate([lru_w_a, lru_w_x], axis=-1).astype(BF16),
        "lru_b_a": vec(lru_b_a), "lru_b_x": vec(lru_b_x), "lru_lambda": vec(lru_lambda),
        "w_ukv": jnp.concatenate([w_uk.reshape(n_l, MLA_KV_RANK, -1), w_uv.reshape(n_l, MLA_KV_RANK, -1)],
                                 axis=-1).astype(BF16),
        "w_out": w_out.astype(BF16),
        "ffn_w_up": ffn_w_up.astype(BF16),
        "ffn_conv_w": ffn_conv_w, "ffn_conv_b": vec(ffn_conv_b),
        "ffn_w_down": ffn_w_down.astype(BF16),
    }

    y_prompt, outs = _run_path(x_prompt.reshape(bsz * seq, d), bsz, seq, mod_p, wts, None)

    past = cache_attn_k.shape[2]
    ctx = {
        "k": cache_attn_k.reshape(dbsz, n_l, past, ATTN_KV_COLS).astype(BF16),
        "v": cache_attn_v.reshape(dbsz, n_l, past, ATTN_KV_COLS).astype(BF16),
        "ckv": cache_mla_ckv.astype(BF16),
        "kr": jnp.pad(cache_mla_krope, ((0, 0), (0, 0), (0, 0), (0, 128 - MLA_ROPE))).astype(BF16),
        "st": state_lru.reshape(dbsz, n_l, 2, 1, LRU_WIDTH),
    }
    y_sample, _ = _run_path(x_sample.reshape(dbsz * dseq, d), dbsz, dseq, mod_s, wts, ctx)

    return (y_prompt, y_sample,
            jnp.stack(outs["k"], axis=1), jnp.stack(outs["v"], axis=1),
            jnp.stack(outs["ckv"], axis=1), jnp.stack(outs["kr"], axis=1),
            jnp.stack(outs["st"], axis=1))
```

```python
import functools

import numpy as np
import jax
import jax.numpy as jnp
from jax import lax
from jax.experimental import pallas as pl
from jax.experimental.pallas import tpu as pltpu

F32 = jnp.float32
BF16 = jnp.bfloat16

GRID_W = 64
ROPE_THETA = 10000.0
NORM_EPS = 1e-6
N_MOD = 6
ATTN_HEADS = 4
ATTN_KV_HEADS = 2
HEAD_DIM = 128
LRU_WIDTH = 1024
LRU_BLOCKS = 8
LRU_BLOCK_W = LRU_WIDTH // LRU_BLOCKS
LRU_CONV_W = 4
LRU_C = 8.0
MLA_HEADS = 4
MLA_NOPE = 128
MLA_ROPE = 64
MLA_V = 128
MLA_KV_RANK = 512
MLA_QPAD = 256

ATTN_Q_COLS = ATTN_HEADS * HEAD_DIM
ATTN_KV_COLS = ATTN_KV_HEADS * HEAD_DIM
MLA_Q_COLS = MLA_HEADS * (MLA_NOPE + MLA_ROPE)
OFF_QA = 0
OFF_KA = OFF_QA + ATTN_Q_COLS
OFF_VA = OFF_KA + ATTN_KV_COLS
OFF_XR = OFF_VA + ATTN_KV_COLS
OFF_GR = OFF_XR + LRU_WIDTH
OFF_QM = OFF_GR + LRU_WIDTH
OFF_CKV = OFF_QM + MLA_Q_COLS
OFF_KR = OFF_CKV + MLA_KV_RANK
D_IN = OFF_KR + MLA_ROPE
P_QA = 0
P_KA = P_QA + ATTN_Q_COLS
P_VA = P_KA + ATTN_KV_COLS
P_QM = P_VA + ATTN_KV_COLS
P_CKV = P_QM + MLA_HEADS * MLA_QPAD
P_KR = P_CKV + MLA_KV_RANK
P_END = P_KR + 128

VMEM_LIMIT_BYTES = 56 * 1024 * 1024


def _tile(n, pref, mult=128):
    if n <= pref:
        return n
    t = (pref // mult) * mult
    while t >= mult:
        if n % t == 0:
            return t
        t -= mult
    return n


def _params(*sem):
    return pltpu.CompilerParams(dimension_semantics=sem, vmem_limit_bytes=VMEM_LIMIT_BYTES)


def _rms(x):
    return x * lax.rsqrt(jnp.mean(x * x, axis=-1, keepdims=True) + NORM_EPS)


def _mod_kernel(c_ref, w_ref, b_ref, o_ref):
    c = c_ref[...]
    s = (c * jax.nn.sigmoid(c)).astype(BF16)
    o_ref[...] = jnp.dot(s, w_ref[...].astype(BF16), preferred_element_type=F32) + b_ref[...]


def _mod_all(cond, w_mod, b_mod):
    n_l, d, n = w_mod.shape
    tn = _tile(n, 1024)
    return pl.pallas_call(
        _mod_kernel,
        grid=(n_l, n // tn),
        in_specs=[
            pl.BlockSpec((8, d), lambda l, j: (0, 0)),
            pl.BlockSpec((None, d, tn), lambda l, j: (l, 0, j)),
            pl.BlockSpec((None, 1, tn), lambda l, j: (l, 0, j)),
        ],
        out_specs=pl.BlockSpec((None, 8, tn), lambda l, j: (l, 0, j)),
        out_shape=jax.ShapeDtypeStruct((n_l, 8, n), F32),
        compiler_params=_params("parallel", "parallel"),
        name="mod_vectors",
    )(cond, w_mod, b_mod.reshape(n_l, 1, n))


def _normmod_kernel(x_ref, mod_ref, g_ref, h_ref):
    m = mod_ref[...]
    h_ref[...] = (_rms(x_ref[...]) * g_ref[...] * (1.0 + m[1:2]) + m[0:1]).astype(BF16)


def _norm_mod(x, mod, g, l):
    m_rows, d = x.shape
    n_mod = mod.shape[1]
    tm = _tile(m_rows // n_mod, 512, 16)
    per = (m_rows // n_mod) // tm
    return pl.pallas_call(
        _normmod_kernel,
        grid=(m_rows // tm,),
        in_specs=[
            pl.BlockSpec((tm, d), lambda i: (i, 0)),
            pl.BlockSpec((None, None, 8, d), lambda i: (l, i // per, 0, 0)),
            pl.BlockSpec((None, 1, d), lambda i: (l, 0, 0)),
        ],
        out_specs=pl.BlockSpec((tm, d), lambda i: (i, 0)),
        out_shape=jax.ShapeDtypeStruct((m_rows, d), BF16),
        compiler_params=_params("parallel"),
        name="norm_mod",
    )(x, mod, g)


def _attn_proj_kernel(*refs, rope, emit_f32):
    h_ref, w_ref, gq_ref, gk_ref, gkv_ref = refs[:5]
    pos = 5
    if rope:
        c128_ref, s128_ref, c64_ref, s64_ref = refs[pos:pos + 4]
        pos += 4
    qa_ref, ka_ref, va_ref, qm_ref, ckv_ref, kr_ref = refs[pos:pos + 6]
    pos += 6
    if emit_f32:
        kf_ref, vf_ref, ckvf_ref, krf_ref = refs[pos:pos + 4]

    acc = jnp.dot(h_ref[...], w_ref[...], preferred_element_type=F32)
    tm = acc.shape[0]
    if rope:
        lane = lax.broadcasted_iota(jnp.int32, (tm, 128), 1)
        hi32 = (lane & 32) != 0
        hi16 = (lane & 16) != 0
        c128, s128 = c128_ref[...], s128_ref[...]
        c64, s64 = c64_ref[...], s64_ref[...]

        def rope128(x):
            return x * c128 + jnp.where(hi32, pltpu.roll(x, 32, 1), pltpu.roll(x, 96, 1)) * s128

        def rope64(x):
            return x * c64 + jnp.where(hi16, pltpu.roll(x, 16, 1), pltpu.roll(x, 112, 1)) * s64

    gq, gk = gq_ref[...], gk_ref[...]
    for hd in range(ATTN_HEADS):
        x = _rms(acc[:, P_QA + hd * HEAD_DIM:P_QA + (hd + 1) * HEAD_DIM]) * gq
        if rope:
            x = rope128(x)
        qa_ref[:, hd * HEAD_DIM:(hd + 1) * HEAD_DIM] = x.astype(BF16)
    for hd in range(ATTN_KV_HEADS):
        sl = slice(hd * HEAD_DIM, (hd + 1) * HEAD_DIM)
        x = _rms(acc[:, P_KA + hd * HEAD_DIM:P_KA + (hd + 1) * HEAD_DIM]) * gk
        if emit_f32:
            kf_ref[:, sl] = x
        if rope:
            x = rope128(x)
        ka_ref[:, sl] = x.astype(BF16)
    v = acc[:, P_VA:P_VA + ATTN_KV_COLS]
    va_ref[...] = v.astype(BF16)
    if emit_f32:
        vf_ref[...] = v
    for hd in range(MLA_HEADS):
        base = P_QM + hd * MLA_QPAD
        qm_ref[:, hd * MLA_QPAD:hd * MLA_QPAD + 128] = acc[:, base:base + 128].astype(BF16)
        x = acc[:, base + 128:base + 256]
        if rope:
            x = rope64(x)
        qm_ref[:, hd * MLA_QPAD + 128:(hd + 1) * MLA_QPAD] = x.astype(BF16)
    ckv = _rms(acc[:, P_CKV:P_CKV + MLA_KV_RANK]) * gkv_ref[...]
    ckv_ref[...] = ckv.astype(BF16)
    if emit_f32:
        ckvf_ref[...] = ckv
    x = acc[:, P_KR:P_KR + 128]
    if emit_f32:
        krf_ref[...] = x[:, :MLA_ROPE]
    if rope:
        x = rope64(x)
    kr_ref[...] = x.astype(BF16)


def _attn_proj(h, w_att, g_q, g_k, g_kv, l, seq, rope_tabs, emit_f32):
    m_rows, d = h.shape
    rope = rope_tabs is not None
    tm = _tile(seq, 512, 16)
    per = seq // tm
    row = lambda i: (i, 0)
    in_specs = [
        pl.BlockSpec((tm, d), row),
        pl.BlockSpec((None, d, P_END), lambda i: (l, 0, 0)),
        pl.BlockSpec((None, 1, HEAD_DIM), lambda i: (l, 0, 0)),
        pl.BlockSpec((None, 1, HEAD_DIM), lambda i: (l, 0, 0)),
        pl.BlockSpec((None, 1, MLA_KV_RANK), lambda i: (l, 0, 0)),
    ]
    args = [h, w_att, g_q, g_k, g_kv]
    if rope:
        in_specs += [pl.BlockSpec((tm, 128), lambda i: (i % per, 0))] * 4
        args += list(rope_tabs)
    widths = [ATTN_Q_COLS, ATTN_KV_COLS, ATTN_KV_COLS, MLA_HEADS * MLA_QPAD, MLA_KV_RANK, 128]
    out_specs = [pl.BlockSpec((tm, w), row) for w in widths]
    out_shape = [jax.ShapeDtypeStruct((m_rows, w), BF16) for w in widths]
    if emit_f32:
        fw = [ATTN_KV_COLS, ATTN_KV_COLS, MLA_KV_RANK, MLA_ROPE]
        out_specs += [pl.BlockSpec((tm, w), row) for w in fw]
        out_shape += [jax.ShapeDtypeStruct((m_rows, w), F32) for w in fw]
    return pl.pallas_call(
        functools.partial(_attn_proj_kernel, rope=rope, emit_f32=emit_f32),
        grid=(m_rows // tm,),
        in_specs=in_specs,
        out_specs=out_specs,
        out_shape=out_shape,
        compiler_params=_params("parallel"),
        name="attn_proj",
    )(*args)


def _lru_proj_kernel(h_ref, w_ref, x_ref, g_ref):
    acc = jnp.dot(h_ref[...], w_ref[...], preferred_element_type=F32)
    x_ref[...] = acc[:, :LRU_WIDTH]
    g_ref[...] = acc[:, LRU_WIDTH:]


def _lru_proj(h, w_lru, l):
    m_rows, d = h.shape
    tm = _tile(m_rows, 512, 16)
    return pl.pallas_call(
        _lru_proj_kernel,
        grid=(m_rows // tm,),
        in_specs=[
            pl.BlockSpec((tm, d), lambda i: (i, 0)),
            pl.BlockSpec((None, d, 2 * LRU_WIDTH), lambda i: (l, 0, 0)),
        ],
        out_specs=[pl.BlockSpec((tm, LRU_WIDTH), lambda i: (i, 0))] * 2,
        out_shape=[jax.ShapeDtypeStruct((m_rows, LRU_WIDTH), F32)] * 2,
        compiler_params=_params("parallel"),
        name="lru_proj",
    )(h, w_lru)


def _attn_kernel(q_ref, k_ref, v_ref, o_ref, *, n_kv, group, dq, dv, tk, scale):
    tq = q_ref.shape[0]
    n_chunks = k_ref.shape[0] // tk
    rows = group * tq
    c2 = scale * np.log2(np.e)
    qs = []
    for hk in range(n_kv):
        parts = [q_ref[:, (hk * group + g) * dq:(hk * group + g + 1) * dq] for g in range(group)]
        qs.append(parts[0] if group == 1 else jnp.concatenate(parts, axis=0))

    def body(c, carry):
        off = pl.multiple_of(c * tk, tk)
        new = []
        for hk in range(n_kv):
            m, l, acc = carry[hk]
            kc = k_ref[pl.ds(off, tk), hk * dq:(hk + 1) * dq]
            vc = v_ref[pl.ds(off, tk), hk * dv:(hk + 1) * dv]
            s = lax.dot_general(qs[hk], kc, (((1,), (1,)), ((), ())), preferred_element_type=F32)
            m_new = jnp.maximum(m, jnp.max(s, axis=-1, keepdims=True))
            p = jnp.exp2((s - m_new) * c2)
            alpha = jnp.exp2((m - m_new) * c2)
            l = alpha * l + jnp.sum(p, axis=-1, keepdims=True)
            acc = alpha * acc + jnp.dot(p.astype(BF16), vc, preferred_element_type=F32)
            new.append((m_new, l, acc))
        return tuple(new)

    init = tuple((jnp.full((rows, 1), -jnp.inf, F32), jnp.zeros((rows, 1), F32), jnp.zeros((rows, dv), F32))
                 for _ in range(n_kv))
    final = lax.fori_loop(0, n_chunks, body, init)
    for hk in range(n_kv):
        _, l, acc = final[hk]
        o = acc / l
        for g in range(group):
            o_ref[:, (hk * group + g) * dv:(hk * group + g + 1) * dv] = o[g * tq:(g + 1) * tq].astype(BF16)


def _attention(q, k, v, n_kv, group, dq, dv, scale):
    bsz, n_q, qw = q.shape
    n_k = k.shape[1]
    tq = _tile(n_q, 256, 16)
    tk = _tile(n_k, 512, 128)
    return pl.pallas_call(
        functools.partial(_attn_kernel, n_kv=n_kv, group=group, dq=dq, dv=dv, tk=tk, scale=scale),
        grid=(bsz, n_q // tq),
        in_specs=[
            pl.BlockSpec((None, tq, qw), lambda b, i: (b, i, 0)),
            pl.BlockSpec((None, n_k, k.shape[2]), lambda b, i: (b, 0, 0)),
            pl.BlockSpec((None, n_k, v.shape[2]), lambda b, i: (b, 0, 0)),
        ],
        out_specs=pl.BlockSpec((None, tq, n_kv * group * dv), lambda b, i: (b, i, 0)),
        out_shape=jax.ShapeDtypeStruct((bsz, n_q, n_kv * group * dv), BF16),
        compiler_params=_params("parallel", "parallel"),
        name="attention",
    )(q, k, v)


def _kv_up_kernel(ckv_ref, kr_ref, w_ref, k_ref, v_ref):
    r = jnp.dot(ckv_ref[...], w_ref[...], preferred_element_type=F32)
    kr = kr_ref[...]
    for hd in range(MLA_HEADS):
        k_ref[:, hd * MLA_QPAD:hd * MLA_QPAD + MLA_NOPE] = r[:, hd * MLA_NOPE:(hd + 1) * MLA_NOPE].astype(BF16)
        k_ref[:, hd * MLA_QPAD + MLA_NOPE:(hd + 1) * MLA_QPAD] = kr
    v_ref[...] = r[:, MLA_HEADS * MLA_NOPE:].astype(BF16)


def _kv_up(ckv, kr, w_ukv, l):
    m_rows = ckv.shape[0]
    tm = _tile(m_rows, 512, 16)
    n_out = MLA_HEADS * (MLA_NOPE + MLA_V)
    return pl.pallas_call(
        _kv_up_kernel,
        grid=(m_rows // tm,),
        in_specs=[
            pl.BlockSpec((tm, MLA_KV_RANK), lambda i: (i, 0)),
            pl.BlockSpec((tm, 128), lambda i: (i, 0)),
            pl.BlockSpec((None, MLA_KV_RANK, n_out), lambda i: (l, 0, 0)),
        ],
        out_specs=[
            pl.BlockSpec((tm, MLA_HEADS * MLA_QPAD), lambda i: (i, 0)),
            pl.BlockSpec((tm, MLA_HEADS * MLA_V), lambda i: (i, 0)),
        ],
        out_shape=[
            jax.ShapeDtypeStruct((m_rows, MLA_HEADS * MLA_QPAD), BF16),
            jax.ShapeDtypeStruct((m_rows, MLA_HEADS * MLA_V), BF16),
        ],
        compiler_params=_params("parallel"),
        name="mla_kv_up",
    )(ckv, kr, w_ukv)


def _lru_kernel(*refs, reverse, combine, n_chunks):
    xp_ref, xm_ref, xn_ref, cw_ref, cb_ref, wax_ref, ba_ref, bx_ref, lam_ref, h0_ref = refs[:10]
    if combine:
        hb_ref, g_ref, y_ref, st_ref, a_s, u_s, h_s, carry = refs[10:]
    else:
        ho_ref, st_ref, a_s, u_s, h_s, carry = refs[10:]
    c = pl.program_id(1)
    cc = (n_chunks - 1 - c) if reverse else c
    tc = xm_ref.shape[0]

    xp = jnp.where(cc > 0, xp_ref[...], 0.0)
    xn = jnp.where(cc < n_chunks - 1, xn_ref[...], 0.0)
    xe = jnp.concatenate([xp, xm_ref[...], xn], axis=0)
    cw = cw_ref[...]
    xc = xe[7:7 + tc] * cw[0:1]
    for j in range(1, LRU_CONV_W):
        xc = xc + xe[7 + j:7 + j + tc] * cw[j:j + 1]
    xc = xc + cb_ref[...]

    lam = lam_ref[...]
    ls_c = LRU_C * (jnp.minimum(lam, 0.0) - jnp.log1p(jnp.exp(-jnp.abs(lam))))
    xcb = xc.astype(BF16)
    for n in range(LRU_BLOCKS):
        sl = slice(n * LRU_BLOCK_W, (n + 1) * LRU_BLOCK_W)
        gg = jnp.dot(xcb[:, sl], wax_ref[n], preferred_element_type=F32)
        r = jax.nn.sigmoid(gg[:, :LRU_BLOCK_W] + ba_ref[:, sl])
        i = jax.nn.sigmoid(gg[:, LRU_BLOCK_W:] + bx_ref[:, sl])
        log_a = r * ls_c[:, sl]
        a = jnp.exp(log_a)
        a_s[:, sl] = a
        u_s[:, sl] = jnp.sqrt(-jnp.tanh(log_a) * (a * a + 1.0)) * (i * xc[:, sl])

    @pl.when(c == 0)
    def _():
        carry[...] = h0_ref[...]

    def body(j, h):
        t = (tc - 1 - j) if reverse else j
        h = a_s[pl.ds(t, 1), :] * h + u_s[pl.ds(t, 1), :]
        h_s[pl.ds(t, 1), :] = h
        return h

    h = lax.fori_loop(0, tc, body, carry[...], unroll=8)
    carry[...] = h
    st_ref[...] = h
    if combine:
        y_ref[...] = ((h_s[...] + hb_ref[...]) * jax.nn.gelu(g_ref[...])).astype(BF16)
    else:
        ho_ref[...] = h_s[...]


def _lru_pass(x_r, conv_w, conv_b, wax, b_a, b_x, lam, h0, l, direction, hb=None, g_r=None):
    bsz, seq, w = x_r.shape
    combine = hb is not None
    reverse = direction == 1
    tc = _tile(seq, 256, 8)
    n_chunks = seq // tc
    t8 = tc // 8
    n8 = seq // 8
    d = direction

    def chunk(c):
        return (n_chunks - 1 - c) if reverse else c

    main = lambda b, c: (b, chunk(c), 0)
    in_specs = [
        pl.BlockSpec((None, 8, w), lambda b, c: (b, jnp.maximum(chunk(c) * t8 - 1, 0), 0)),
        pl.BlockSpec((None, tc, w), main),
        pl.BlockSpec((None, 8, w), lambda b, c: (b, jnp.minimum((chunk(c) + 1) * t8, n8 - 1), 0)),
        pl.BlockSpec((None, LRU_CONV_W, w), lambda b, c: (l, 0, 0)),
        pl.BlockSpec((None, 1, w), lambda b, c: (l, 0, 0)),
        pl.BlockSpec((None, None, LRU_BLOCKS, LRU_BLOCK_W, 2 * LRU_BLOCK_W), lambda b, c: (l, d, 0, 0, 0)),
        pl.BlockSpec((None, None, 1, w), lambda b, c: (l, d, 0, 0)),
        pl.BlockSpec((None, None, 1, w), lambda b, c: (l, d, 0, 0)),
        pl.BlockSpec((None, None, 1, w), lambda b, c: (l, d, 0, 0)),
        pl.BlockSpec((None, None, 1, w), lambda b, c: (b, d, 0, 0)),
    ]
    args = [x_r, x_r, x_r, conv_w, conv_b, wax, b_a, b_x, lam, h0]
    st_spec = pl.BlockSpec((None, 1, w), lambda b, c: (b, 0, 0))
    st_shape = jax.ShapeDtypeStruct((bsz, 1, w), F32)
    if combine:
        in_specs += [pl.BlockSpec((None, tc, w), main), pl.BlockSpec((None, tc, w), main)]
        args += [hb, g_r]
        out_specs = [pl.BlockSpec((None, tc, w), main), st_spec]
        out_shape = [jax.ShapeDtypeStruct((bsz, seq, w), BF16), st_shape]
    else:
        out_specs = [pl.BlockSpec((None, tc, w), main), st_spec]
        out_shape = [jax.ShapeDtypeStruct((bsz, seq, w), F32), st_shape]
    return pl.pallas_call(
        functools.partial(_lru_kernel, reverse=reverse, combine=combine, n_chunks=n_chunks),
        grid=(bsz, n_chunks),
        in_specs=in_specs,
        out_specs=out_specs,
        out_shape=out_shape,
        scratch_shapes=[pltpu.VMEM((tc, w), F32)] * 3 + [pltpu.VMEM((1, w), F32)],
        compiler_params=_params("parallel", "arbitrary"),
        name="rglru_bwd" if reverse else "rglru_fwd",
    )(*args)


def _mix_out_kernel(oa_ref, yr_ref, om_ref, w_ref, x_ref, mod_ref, gpost_ref, gffn_ref, xo_ref, h_ref):
    n_a = oa_ref.shape[1]
    n_r = yr_ref.shape[1]
    out = jnp.dot(oa_ref[...], w_ref[0:n_a, :], preferred_element_type=F32)
    out = out + jnp.dot(yr_ref[...], w_ref[n_a:n_a + n_r, :], preferred_element_type=F32)
    out = out + jnp.dot(om_ref[...], w_ref[n_a + n_r:, :], preferred_element_type=F32)
    m = mod_ref[...]
    xn = x_ref[...] + m[2:3] * (_rms(out) * gpost_ref[...])
    xo_ref[...] = xn
    h_ref[...] = (_rms(xn) * gffn_ref[...] * (1.0 + m[4:5]) + m[3:4]).astype(BF16)


def _mix_out(o_a, y_r, o_m, w_out, x, mod, g_post, g_ffn, l):
    m_rows, d = x.shape
    n_mod = mod.shape[1]
    tm = _tile(m_rows // n_mod, 512, 16)
    per = (m_rows // n_mod) // tm
    row = lambda i: (i, 0)
    d_mix = w_out.shape[1]
    return pl.pallas_call(
        _mix_out_kernel,
        grid=(m_rows // tm,),
        in_specs=[
            pl.BlockSpec((tm, o_a.shape[1]), row),
            pl.BlockSpec((tm, y_r.shape[1]), row),
            pl.BlockSpec((tm, o_m.shape[1]), row),
            pl.BlockSpec((None, d_mix, d), lambda i: (l, 0, 0)),
            pl.BlockSpec((tm, d), row),
            pl.BlockSpec((None, None, 8, d), lambda i: (l, i // per, 0, 0)),
            pl.BlockSpec((None, 1, d), lambda i: (l, 0, 0)),
            pl.BlockSpec((None, 1, d), lambda i: (l, 0, 0)),
        ],
        out_specs=[pl.BlockSpec((tm, d), row), pl.BlockSpec((tm, d), row)],
        out_shape=[jax.ShapeDtypeStruct((m_rows, d), F32), jax.ShapeDtypeStruct((m_rows, d), BF16)],
        compiler_params=_params("parallel"),
        name="mix_out",
    )(o_a, y_r, o_m, w_out, x, mod, g_post, g_ffn)


HALO = 16


def _ffn_up_kernel(hp_ref, hm_ref, hn_ref, wg_ref, wv_ref, cwg_ref, cwv_ref, cbg_ref, cbv_ref, o_ref, he_ref, *, seq, seg, sub):
    i = pl.program_id(0)
    tm = hm_ref.shape[0]
    n_seg = tm // seg
    rows = he_ref.shape[0]

    @pl.when(pl.program_id(1) == 0)
    def _():
        for s in range(n_seg):
            o = HALO + s * (seg + HALO)
            he_ref[o:o + seg, :] = hm_ref[s * seg:(s + 1) * seg, :]
            if s > 0:
                he_ref[o - HALO:o, :] = jnp.zeros((HALO, he_ref.shape[1]), BF16)
        if n_seg == 1:
            he_ref[0:HALO, :] = jnp.where((i * tm) % seq != 0, hp_ref[...], jnp.zeros_like(hp_ref))
            he_ref[rows - HALO:, :] = jnp.where(((i + 1) * tm) % seq != 0, hn_ref[...], jnp.zeros_like(hn_ref))
        else:
            he_ref[0:HALO, :] = jnp.zeros((HALO, he_ref.shape[1]), BF16)
            he_ref[rows - HALO:, :] = jnp.zeros((HALO, he_ref.shape[1]), BF16)

    he = he_ref[...]

    def branch(w_ref, cw_ref, cb_ref, c0):
        u = jnp.dot(he, w_ref[:, c0:c0 + sub], preferred_element_type=F32)
        u_prev = pltpu.roll(u, 1, 0)
        u_next = pltpu.roll(u, rows - 1, 0)
        cw = cw_ref[:, c0:c0 + sub]
        cb = cb_ref[:, c0:c0 + sub]
        outs = []
        for s in range(n_seg):
            o = HALO + s * (seg + HALO)
            outs.append(u_prev[o:o + seg] * cw[0:1] + u[o:o + seg] * cw[1:2] + u_next[o:o + seg] * cw[2:3] + cb)
        return outs[0] if n_seg == 1 else jnp.concatenate(outs, axis=0)

    for c0 in range(0, o_ref.shape[1], sub):
        g = branch(wg_ref, cwg_ref, cbg_ref, c0)
        v = branch(wv_ref, cwv_ref, cbv_ref, c0)
        o_ref[:, c0:c0 + sub] = (g * jax.nn.sigmoid(g) * v).astype(BF16)


def _ffn_up(h, w_up, conv_w, conv_b, l, seq):
    m_rows, d = h.shape
    d_ff = w_up.shape[2] // 2
    tm = _tile(m_rows, 1024, 16)
    if tm % seq != 0 and seq % tm != 0:
        tm = _tile(seq, 1024, 16)
    seg = min(tm, seq)
    tn = _tile(d_ff, 512)
    sub = tn
    nj = d_ff // tn
    th = tm // HALO
    nh = m_rows // HALO
    he_rows = (tm // seg) * (seg + HALO) + HALO
    return pl.pallas_call(
        functools.partial(_ffn_up_kernel, seq=seq, seg=seg, sub=sub),
        grid=(m_rows // tm, nj),
        in_specs=[
            pl.BlockSpec((HALO, d), lambda i, j: (jnp.maximum(i * th - 1, 0), 0)),
            pl.BlockSpec((tm, d), lambda i, j: (i, 0)),
            pl.BlockSpec((HALO, d), lambda i, j: (jnp.minimum((i + 1) * th, nh - 1), 0)),
            pl.BlockSpec((None, d, tn), lambda i, j: (l, 0, j)),
            pl.BlockSpec((None, d, tn), lambda i, j: (l, 0, j + nj)),
            pl.BlockSpec((None, 3, tn), lambda i, j: (l, 0, j)),
            pl.BlockSpec((None, 3, tn), lambda i, j: (l, 0, j + nj)),
            pl.BlockSpec((None, 1, tn), lambda i, j: (l, 0, j)),
            pl.BlockSpec((None, 1, tn), lambda i, j: (l, 0, j + nj)),
        ],
        out_specs=pl.BlockSpec((tm, tn), lambda i, j: (i, j)),
        out_shape=jax.ShapeDtypeStruct((m_rows, d_ff), BF16),
        scratch_shapes=[pltpu.VMEM((he_rows, d), BF16)],
        compiler_params=_params("parallel", "arbitrary"),
        name="ffn_up",
    )(h, h, h, w_up, w_up, conv_w, conv_w, conv_b, conv_b)


def _ffn_down_kernel(*refs, has_next):
    a_ref, w_ref, x_ref, mod_ref, gpost_ref = refs[:5]
    if has_next:
        modn_ref, gnext_ref, xo_ref, hn_ref, acc_ref = refs[5:]
    else:
        xo_ref, acc_ref = refs[5:]
    k = pl.program_id(1)

    @pl.when(k == 0)
    def _():
        acc_ref[...] = jnp.zeros_like(acc_ref)

    acc_ref[...] += jnp.dot(a_ref[...], w_ref[...], preferred_element_type=F32)

    @pl.when(k == pl.num_programs(1) - 1)
    def _():
        m = mod_ref[...]
        xn = x_ref[...] + m[5:6] * (_rms(acc_ref[...]) * gpost_ref[...])
        xo_ref[...] = xn
        if has_next:
            mn = modn_ref[...]
            hn_ref[...] = (_rms(xn) * gnext_ref[...] * (1.0 + mn[1:2]) + mn[0:1]).astype(BF16)


def _ffn_down(act, w_down, x, mod, g_post, g_pre_mix, l, has_next):
    m_rows, d = x.shape
    d_ff = act.shape[1]
    n_mod = mod.shape[1]
    tm = _tile(m_rows // n_mod, 512, 16)
    per = (m_rows // n_mod) // tm
    tk = _tile(d_ff, 1408)
    row = lambda i, k: (i, 0)
    in_specs = [
        pl.BlockSpec((tm, tk), lambda i, k: (i, k)),
        pl.BlockSpec((None, tk, d), lambda i, k: (l, k, 0)),
        pl.BlockSpec((tm, d), row),
        pl.BlockSpec((None, None, 8, d), lambda i, k: (l, i // per, 0, 0)),
        pl.BlockSpec((None, 1, d), lambda i, k: (l, 0, 0)),
    ]
    args = [act, w_down, x, mod, g_post]
    out_specs = [pl.BlockSpec((tm, d), row)]
    out_shape = [jax.ShapeDtypeStruct((m_rows, d), F32)]
    if has_next:
        in_specs += [
            pl.BlockSpec((None, None, 8, d), lambda i, k: (l + 1, i // per, 0, 0)),
            pl.BlockSpec((None, 1, d), lambda i, k: (l + 1, 0, 0)),
        ]
        args += [mod, g_pre_mix]
        out_specs.append(pl.BlockSpec((tm, d), row))
        out_shape.append(jax.ShapeDtypeStruct((m_rows, d), BF16))
    res = pl.pallas_call(
        functools.partial(_ffn_down_kernel, has_next=has_next),
        grid=(m_rows // tm, d_ff // tk),
        in_specs=in_specs,
        out_specs=out_specs,
        out_shape=out_shape,
        scratch_shapes=[pltpu.VMEM((tm, d), F32)],
        compiler_params=_params("parallel", "arbitrary"),
        name="ffn_down",
    )(*args)
    return (res[0], res[1]) if has_next else (res[0], None)


def _rope_tables(n_tok):
    rows = n_tok // GRID_W
    row = jnp.repeat(jnp.arange(rows, dtype=jnp.int32), GRID_W)
    col = jnp.tile(jnp.arange(GRID_W, dtype=jnp.int32), rows)

    def tabs(rot_dim):
        quarter = rot_dim // 4
        inv_freq = ROPE_THETA ** (-jnp.arange(quarter, dtype=F32) / quarter)
        cs, ss = [], []
        for p in (row, col):
            ang = p.astype(F32)[:, None] * inv_freq[None, :]
            cs += [jnp.cos(ang), jnp.cos(ang)]
            ss += [-jnp.sin(ang), jnp.sin(ang)]
        return jnp.concatenate(cs, axis=-1), jnp.concatenate(ss, axis=-1)

    c128, s128 = tabs(HEAD_DIM)
    c64, s64 = tabs(MLA_ROPE)
    pad = ((0, 0), (0, 128 - MLA_ROPE))
    return c128, s128, jnp.pad(c64, pad), jnp.pad(s64, pad)


def _run_path(x, bsz, seq, mod, wts, ctx):
    n_l = wts["w_att"].shape[0]
    d = x.shape[1]
    latent = ctx is not None
    rope_tabs = _rope_tables(seq) if latent else None
    outs = {"k": [], "v": [], "ckv": [], "kr": [], "st": []}
    h = _norm_mod(x, mod, wts["g_pre_mix"], 0)
    for l in range(n_l):
        proj = _attn_proj(h, wts["w_att"], wts["g_q"], wts["g_k"], wts["g_kv"], l, seq, rope_tabs, not latent)
        q_a, k_a, v_a, q_m, ckv, k_r = proj[:6]
        x_r, g_r = _lru_proj(h, wts["w_lru"], l)

        k_all = k_a.reshape(bsz, seq, ATTN_KV_COLS)
        v_all = v_a.reshape(bsz, seq, ATTN_KV_COLS)
        ckv_all = ckv.reshape(bsz, seq, MLA_KV_RANK)
        kr_all = k_r.reshape(bsz, seq, 128)
        if latent:
            k_all = jnp.concatenate([ctx["k"][:, l], k_all], axis=1)
            v_all = jnp.concatenate([ctx["v"][:, l], v_all], axis=1)
            ckv_all = jnp.concatenate([ctx["ckv"][:, l], ckv_all], axis=1)
            kr_all = jnp.concatenate([ctx["kr"][:, l], kr_all], axis=1)
            h0 = ctx["st"]
            h0_l = l
        else:
            kf, vf, ckvf, krf = proj[6:]
            outs["k"].append(kf.reshape(bsz, seq, ATTN_KV_HEADS, HEAD_DIM))
            outs["v"].append(vf.reshape(bsz, seq, ATTN_KV_HEADS, HEAD_DIM))
            outs["ckv"].append(ckvf.reshape(bsz, seq, MLA_KV_RANK))
            outs["kr"].append(krf.reshape(bsz, seq, MLA_ROPE))
            h0 = jnp.zeros((bsz, 1, 2, 1, LRU_WIDTH), F32)
            h0_l = 0
        n_k = k_all.shape[1]

        o_a = _attention(q_a.reshape(bsz, seq, ATTN_Q_COLS), k_all, v_all,
                         ATTN_KV_HEADS, ATTN_HEADS // ATTN_KV_HEADS, HEAD_DIM, HEAD_DIM, HEAD_DIM ** -0.5)
        k_m, v_m = _kv_up(ckv_all.reshape(bsz * n_k, MLA_KV_RANK), kr_all.reshape(bsz * n_k, 128), wts["w_ukv"], l)
        o_m = _attention(q_m.reshape(bsz, seq, MLA_HEADS * MLA_QPAD),
                         k_m.reshape(bsz, n_k, MLA_HEADS * MLA_QPAD), v_m.reshape(bsz, n_k, MLA_HEADS * MLA_V),
                         MLA_HEADS, 1, MLA_QPAD, MLA_V, (MLA_NOPE + MLA_ROPE) ** -0.5)

        x_r3 = x_r.reshape(bsz, seq, LRU_WIDTH)
        lru_args = (x_r3, wts["lru_conv_w"], wts["lru_conv_b"], wts["lru_wax"], wts["lru_b_a"], wts["lru_b_x"],
                    wts["lru_lambda"], h0[:, h0_l])
        hb, st_b = _lru_pass(*lru_args, l, 1)
        y_r, st_f = _lru_pass(*lru_args, l, 0, hb=hb, g_r=g_r.reshape(bsz, seq, LRU_WIDTH))
        if not latent:
            outs["st"].append(jnp.concatenate([st_f, st_b], axis=1))

        x, h2 = _mix_out(o_a.reshape(bsz * seq, -1), y_r.reshape(bsz * seq, LRU_WIDTH), o_m.reshape(bsz * seq, -1),
                         wts["w_out"], x, mod, wts["g_post_mix"], wts["g_pre_ffn"], l)
        act = _ffn_up(h2, wts["ffn_w_up"], wts["ffn_conv_w"], wts["ffn_conv_b"], l, seq)
        x, h = _ffn_down(act, wts["ffn_w_down"], x, mod, wts["g_post_ffn"], wts["g_pre_mix"], l, l + 1 < n_l)
    return x.reshape(bsz, seq, d), outs


def kernel(x_prompt, x_sample, cache_attn_k, cache_attn_v, cache_mla_ckv, cache_mla_krope, state_lru, c, c_ctx, w_mod, b_mod, g_pre_mix, g_post_mix, g_pre_ffn, g_post_ffn, w_in, g_q, g_k, lru_conv_w, lru_conv_b, lru_w_a, lru_b_a, lru_w_x, lru_b_x, lru_lambda, g_kv, w_uk, w_uv, w_out, ffn_w_up, ffn_conv_w, ffn_conv_b, ffn_w_down):
    n_l, d, _ = w_in.shape
    bsz, seq, _ = x_prompt.shape
    dbsz, dseq, _ = x_sample.shape
    assert dbsz <= 7 and w_in.shape[2] == D_IN

    cond = jnp.zeros((8, d), F32).at[0].set(c_ctx).at[1:1 + dbsz].set(c)
    mod_all = _mod_all(cond, w_mod, b_mod).reshape(n_l, 8, N_MOD, d)
    mod_all = jnp.pad(mod_all, ((0, 0), (0, 0), (0, 8 - N_MOD), (0, 0)))
    mod_p = mod_all[:, 0:1]
    mod_s = mod_all[:, 1:1 + dbsz]

    qm = w_in[:, :, OFF_QM:OFF_CKV].reshape(n_l, d, MLA_HEADS, MLA_NOPE + MLA_ROPE)
    qm = jnp.pad(qm, ((0, 0), (0, 0), (0, 0), (0, MLA_QPAD - MLA_NOPE - MLA_ROPE))).reshape(n_l, d, MLA_HEADS * MLA_QPAD)
    w_att = jnp.concatenate(
        [w_in[:, :, OFF_QA:OFF_XR], qm, w_in[:, :, OFF_CKV:OFF_KR],
         jnp.pad(w_in[:, :, OFF_KR:], ((0, 0), (0, 0), (0, 128 - MLA_ROPE)))], axis=-1).astype(BF16)
    vec = lambda a: a.reshape(a.shape[:-1] + (1, a.shape[-1]))
    wts = {
        "w_att": w_att,
        "w_lru": w_in[:, :, OFF_XR:OFF_QM].astype(BF16),
        "g_q": vec(g_q), "g_k": vec(g_k), "g_kv": vec(g_kv),
        "g_pre_mix": vec(g_pre_mix), "g_post_mix": vec(g_post_mix),
        "g_pre_ffn": vec(g_pre_ffn), "g_post_ffn": vec(g_post_ffn),
        "lru_conv_w": lru_conv_w, "lru_conv_b": vec(lru_conv_b),
        "lru_wax": jnp.concatenate([lru_w_a, lru_w_x], axis=-1).astype(BF16),
        "lru_b_a": vec(lru_b_a), "lru_b_x": vec(lru_b_x), "lru_lambda": vec(lru_lambda),
        "w_ukv": jnp.concatenate([w_uk.reshape(n_l, MLA_KV_RANK, -1), w_uv.reshape(n_l, MLA_KV_RANK, -1)],
                                 axis=-1).astype(BF16),
        "w_out": w_out.astype(BF16),
        "ffn_w_up": ffn_w_up.astype(BF16),
        "ffn_conv_w": ffn_conv_w, "ffn_conv_b": vec(ffn_conv_b),
        "ffn_w_down": ffn_w_down.astype(BF16),
    }

    y_prompt, outs = _run_path(x_prompt.reshape(bsz * seq, d), bsz, seq, mod_p, wts, None)

    past = cache_attn_k.shape[2]
    ctx = {
        "k": cache_attn_k.reshape(dbsz, n_l, past, ATTN_KV_COLS).astype(BF16),
        "v": cache_attn_v.reshape(dbsz, n_l, past, ATTN_KV_COLS).astype(BF16),
        "ckv": cache_mla_ckv.astype(BF16),
        "kr": jnp.pad(cache_mla_krope, ((0, 0), (0, 0), (0, 0), (0, 128 - MLA_ROPE))).astype(BF16),
        "st": state_lru.reshape(dbsz, n_l, 2, 1, LRU_WIDTH),
    }
    y_sample, _ = _run_path(x_sample.reshape(dbsz * dseq, d), dbsz, dseq, mod_s, wts, ctx)

    return (y_prompt, y_sample,
            jnp.stack(outs["k"], axis=1), jnp.stack(outs["v"], axis=1),
            jnp.stack(outs["ckv"], axis=1), jnp.stack(outs["kr"], axis=1),
            jnp.stack(outs["st"], axis=1))
```

```python
import functools

import numpy as np
import jax
import jax.numpy as jnp
from jax import lax
from jax.experimental import pallas as pl
from jax.experimental.pallas import tpu as pltpu

F32 = jnp.float32
BF16 = jnp.bfloat16

GRID_W = 64
ROPE_THETA = 10000.0
NORM_EPS = 1e-6
N_MOD = 6
ATTN_HEADS = 4
ATTN_KV_HEADS = 2
HEAD_DIM = 128
LRU_WIDTH = 1024
LRU_BLOCKS = 8
LRU_BLOCK_W = LRU_WIDTH // LRU_BLOCKS
LRU_CONV_W = 4
LRU_C = 8.0
MLA_HEADS = 4
MLA_NOPE = 128
MLA_ROPE = 64
MLA_V = 128
MLA_KV_RANK = 512
MLA_QPAD = 256

ATTN_Q_COLS = ATTN_HEADS * HEAD_DIM
ATTN_KV_COLS = ATTN_KV_HEADS * HEAD_DIM
MLA_Q_COLS = MLA_HEADS * (MLA_NOPE + MLA_ROPE)
OFF_QA = 0
OFF_KA = OFF_QA + ATTN_Q_COLS
OFF_VA = OFF_KA + ATTN_KV_COLS
OFF_XR = OFF_VA + ATTN_KV_COLS
OFF_GR = OFF_XR + LRU_WIDTH
OFF_QM = OFF_GR + LRU_WIDTH
OFF_CKV = OFF_QM + MLA_Q_COLS
OFF_KR = OFF_CKV + MLA_KV_RANK
D_IN = OFF_KR + MLA_ROPE
P_QA = 0
P_KA = P_QA + ATTN_Q_COLS
P_VA = P_KA + ATTN_KV_COLS
P_QM = P_VA + ATTN_KV_COLS
P_CKV = P_QM + MLA_HEADS * MLA_QPAD
P_KR = P_CKV + MLA_KV_RANK
P_END = P_KR + 128

VMEM_LIMIT_BYTES = 56 * 1024 * 1024


def _tile(n, pref, mult=128):
    if n <= pref:
        return n
    t = (pref // mult) * mult
    while t >= mult:
        if n % t == 0:
            return t
        t -= mult
    return n


def _params(*sem):
    return pltpu.CompilerParams(dimension_semantics=sem, vmem_limit_bytes=VMEM_LIMIT_BYTES)


def _rms(x):
    return x * lax.rsqrt(jnp.mean(x * x, axis=-1, keepdims=True) + NORM_EPS)


def _mod_kernel(c_ref, w_ref, b_ref, o_ref):
    c = c_ref[...]
    s = (c * jax.nn.sigmoid(c)).astype(BF16)
    o_ref[...] = jnp.dot(s, w_ref[...].astype(BF16), preferred_element_type=F32) + b_ref[...]


def _mod_all(cond, w_mod, b_mod):
    n_l, d, n = w_mod.shape
    tn = _tile(n, 1024)
    return pl.pallas_call(
        _mod_kernel,
        grid=(n_l, n // tn),
        in_specs=[
            pl.BlockSpec((8, d), lambda l, j: (0, 0)),
            pl.BlockSpec((None, d, tn), lambda l, j: (l, 0, j)),
            pl.BlockSpec((None, 1, tn), lambda l, j: (l, 0, j)),
        ],
        out_specs=pl.BlockSpec((None, 8, tn), lambda l, j: (l, 0, j)),
        out_shape=jax.ShapeDtypeStruct((n_l, 8, n), F32),
        compiler_params=_params("parallel", "parallel"),
        name="mod_vectors",
    )(cond, w_mod, b_mod.reshape(n_l, 1, n))


def _normmod_kernel(x_ref, mod_ref, g_ref, h_ref):
    m = mod_ref[...]
    h_ref[...] = (_rms(x_ref[...]) * g_ref[...] * (1.0 + m[1:2]) + m[0:1]).astype(BF16)


def _norm_mod(x, mod, g, l):
    m_rows, d = x.shape
    n_mod = mod.shape[1]
    tm = _tile(m_rows // n_mod, 512, 16)
    per = (m_rows // n_mod) // tm
    return pl.pallas_call(
        _normmod_kernel,
        grid=(m_rows // tm,),
        in_specs=[
            pl.BlockSpec((tm, d), lambda i: (i, 0)),
            pl.BlockSpec((None, None, 8, d), lambda i: (l, i // per, 0, 0)),
            pl.BlockSpec((None, 1, d), lambda i: (l, 0, 0)),
        ],
        out_specs=pl.BlockSpec((tm, d), lambda i: (i, 0)),
        out_shape=jax.ShapeDtypeStruct((m_rows, d), BF16),
        compiler_params=_params("parallel"),
        name="norm_mod",
    )(x, mod, g)


def _attn_proj_kernel(*refs, rope, emit_f32):
    h_ref, w_ref, gq_ref, gk_ref, gkv_ref = refs[:5]
    pos = 5
    if rope:
        c128_ref, s128_ref, c64_ref, s64_ref = refs[pos:pos + 4]
        pos += 4
    qa_ref, ka_ref, va_ref, qm_ref, ckv_ref, kr_ref = refs[pos:pos + 6]
    pos += 6
    if emit_f32:
        kf_ref, vf_ref, ckvf_ref, krf_ref = refs[pos:pos + 4]

    acc = jnp.dot(h_ref[...], w_ref[...], preferred_element_type=F32)
    tm = acc.shape[0]
    if rope:
        lane = lax.broadcasted_iota(jnp.int32, (tm, 128), 1)
        hi32 = (lane & 32) != 0
        hi16 = (lane & 16) != 0
        c128, s128 = c128_ref[...], s128_ref[...]
        c64, s64 = c64_ref[...], s64_ref[...]

        def rope128(x):
            return x * c128 + jnp.where(hi32, pltpu.roll(x, 32, 1), pltpu.roll(x, 96, 1)) * s128

        def rope64(x):
            return x * c64 + jnp.where(hi16, pltpu.roll(x, 16, 1), pltpu.roll(x, 112, 1)) * s64

    gq, gk = gq_ref[...], gk_ref[...]
    for hd in range(ATTN_HEADS):
        x = _rms(acc[:, P_QA + hd * HEAD_DIM:P_QA + (hd + 1) * HEAD_DIM]) * gq
        if rope:
            x = rope128(x)
        qa_ref[:, hd * HEAD_DIM:(hd + 1) * HEAD_DIM] = x.astype(BF16)
    for hd in range(ATTN_KV_HEADS):
        sl = slice(hd * HEAD_DIM, (hd + 1) * HEAD_DIM)
        x = _rms(acc[:, P_KA + hd * HEAD_DIM:P_KA + (hd + 1) * HEAD_DIM]) * gk
        if emit_f32:
            kf_ref[:, sl] = x
        if rope:
            x = rope128(x)
        ka_ref[:, sl] = x.astype(BF16)
    v = acc[:, P_VA:P_VA + ATTN_KV_COLS]
    va_ref[...] = v.astype(BF16)
    if emit_f32:
        vf_ref[...] = v
    for hd in range(MLA_HEADS):
        base = P_QM + hd * MLA_QPAD
        qm_ref[:, hd * MLA_QPAD:hd * MLA_QPAD + 128] = acc[:, base:base + 128].astype(BF16)
        x = acc[:, base + 128:base + 256]
        if rope:
            x = rope64(x)
        qm_ref[:, hd * MLA_QPAD + 128:(hd + 1) * MLA_QPAD] = x.astype(BF16)
    ckv = _rms(acc[:, P_CKV:P_CKV + MLA_KV_RANK]) * gkv_ref[...]
    ckv_ref[...] = ckv.astype(BF16)
    if emit_f32:
        ckvf_ref[...] = ckv
    x = acc[:, P_KR:P_KR + 128]
    if emit_f32:
        krf_ref[...] = x[:, :MLA_ROPE]
    if rope:
        x = rope64(x)
    kr_ref[...] = x.astype(BF16)


def _attn_proj(h, w_att, g_q, g_k, g_kv, l, seq, rope_tabs, emit_f32):
    m_rows, d = h.shape
    rope = rope_tabs is not None
    tm = _tile(seq, 512, 16)
    per = seq // tm
    row = lambda i: (i, 0)
    in_specs = [
        pl.BlockSpec((tm, d), row),
        pl.BlockSpec((None, d, P_END), lambda i: (l, 0, 0)),
        pl.BlockSpec((None, 1, HEAD_DIM), lambda i: (l, 0, 0)),
        pl.BlockSpec((None, 1, HEAD_DIM), lambda i: (l, 0, 0)),
        pl.BlockSpec((None, 1, MLA_KV_RANK), lambda i: (l, 0, 0)),
    ]
    args = [h, w_att, g_q, g_k, g_kv]
    if rope:
        in_specs += [pl.BlockSpec((tm, 128), lambda i: (i % per, 0))] * 4
        args += list(rope_tabs)
    widths = [ATTN_Q_COLS, ATTN_KV_COLS, ATTN_KV_COLS, MLA_HEADS * MLA_QPAD, MLA_KV_RANK, 128]
    out_specs = [pl.BlockSpec((tm, w), row) for w in widths]
    out_shape = [jax.ShapeDtypeStruct((m_rows, w), BF16) for w in widths]
    if emit_f32:
        fw = [ATTN_KV_COLS, ATTN_KV_COLS, MLA_KV_RANK, MLA_ROPE]
        out_specs += [pl.BlockSpec((tm, w), row) for w in fw]
        out_shape += [jax.ShapeDtypeStruct((m_rows, w), F32) for w in fw]
    return pl.pallas_call(
        functools.partial(_attn_proj_kernel, rope=rope, emit_f32=emit_f32),
        grid=(m_rows // tm,),
        in_specs=in_specs,
        out_specs=out_specs,
        out_shape=out_shape,
        compiler_params=_params("parallel"),
        name="attn_proj",
    )(*args)


def _lru_proj_kernel(h_ref, w_ref, x_ref, g_ref):
    acc = jnp.dot(h_ref[...], w_ref[...], preferred_element_type=F32)
    x_ref[...] = acc[:, :LRU_WIDTH]
    g_ref[...] = acc[:, LRU_WIDTH:]


def _lru_proj(h, w_lru, l):
    m_rows, d = h.shape
    tm = _tile(m_rows, 512, 16)
    return pl.pallas_call(
        _lru_proj_kernel,
        grid=(m_rows // tm,),
        in_specs=[
            pl.BlockSpec((tm, d), lambda i: (i, 0)),
            pl.BlockSpec((None, d, 2 * LRU_WIDTH), lambda i: (l, 0, 0)),
        ],
        out_specs=[pl.BlockSpec((tm, LRU_WIDTH), lambda i: (i, 0))] * 2,
        out_shape=[jax.ShapeDtypeStruct((m_rows, LRU_WIDTH), F32)] * 2,
        compiler_params=_params("parallel"),
        name="lru_proj",
    )(h, w_lru)


LANES = 128
STRIP = 32


def _attn_kernel(q_ref, k_ref, v_ref, o_ref, *scratch, n_kv, group, dq, dv, tk, scale):
    tq = q_ref.shape[0]
    n_chunks = k_ref.shape[0] // tk
    rows = group * tq
    strip = STRIP if rows % STRIP == 0 else rows
    reps = tk // LANES
    c2 = scale * np.log2(np.e)
    heads = [scratch[6 * hk:6 * hk + 6] for hk in range(n_kv)]

    for hk, (qs_ref, s_ref, p_ref, acc_ref, m_ref, l_ref) in enumerate(heads):
        for g in range(group):
            qs_ref[g * tq:(g + 1) * tq, :] = q_ref[:, (hk * group + g) * dq:(hk * group + g + 1) * dq]
        m_ref[...] = jnp.full(m_ref.shape, -jnp.inf, F32)
        l_ref[...] = jnp.zeros(l_ref.shape, F32)
        acc_ref[...] = jnp.zeros(acc_ref.shape, F32)

    def body(c, carry):
        off = pl.multiple_of(c * tk, tk)
        for hk, (qs_ref, s_ref, p_ref, acc_ref, m_ref, l_ref) in enumerate(heads):
            kc = k_ref[pl.ds(off, tk), hk * dq:(hk + 1) * dq]
            s_ref[...] = lax.dot_general(qs_ref[...], kc, (((1,), (1,)), ((), ())), preferred_element_type=F32)
        for hk, (qs_ref, s_ref, p_ref, acc_ref, m_ref, l_ref) in enumerate(heads):
            for r0 in range(0, rows, strip):
                r = slice(r0, r0 + strip)
                s = s_ref[r, :]
                m_prev = m_ref[r, :]
                m_new = jnp.maximum(m_prev, jnp.max(s, axis=-1, keepdims=True))
                p = jnp.exp2((s - jnp.concatenate([m_new] * reps, axis=1)) * c2)
                alpha = jnp.exp2((m_prev - m_new) * c2)
                l_ref[r, :] = alpha * l_ref[r, :] + jnp.sum(p, axis=-1, keepdims=True)
                m_ref[r, :] = m_new
                acc_ref[r, :] = alpha * acc_ref[r, :]
                p_ref[r, :] = p.astype(BF16)
            vc = v_ref[pl.ds(off, tk), hk * dv:(hk + 1) * dv]
            acc_ref[...] += jnp.dot(p_ref[...], vc, preferred_element_type=F32)
        return carry

    lax.fori_loop(0, n_chunks, body, 0)
    for hk, (qs_ref, s_ref, p_ref, acc_ref, m_ref, l_ref) in enumerate(heads):
        o = acc_ref[...] / l_ref[...]
        for g in range(group):
            o_ref[:, (hk * group + g) * dv:(hk * group + g + 1) * dv] = o[g * tq:(g + 1) * tq].astype(BF16)


def _attention(q, k, v, n_kv, group, dq, dv, scale):
    bsz, n_q, qw = q.shape
    n_k = k.shape[1]
    tq = _tile(n_q, 256, 16)
    tk = _tile(n_k, 512, 128)
    rows = group * tq
    assert dv == LANES and tk % LANES == 0
    head_scratch = [pltpu.VMEM((rows, dq), BF16), pltpu.VMEM((rows, tk), F32), pltpu.VMEM((rows, tk), BF16),
                    pltpu.VMEM((rows, dv), F32), pltpu.VMEM((rows, LANES), F32), pltpu.VMEM((rows, LANES), F32)]
    return pl.pallas_call(
        functools.partial(_attn_kernel, n_kv=n_kv, group=group, dq=dq, dv=dv, tk=tk, scale=scale),
        grid=(bsz, n_q // tq),
        in_specs=[
            pl.BlockSpec((None, tq, qw), lambda b, i: (b, i, 0)),
            pl.BlockSpec((None, n_k, k.shape[2]), lambda b, i: (b, 0, 0)),
            pl.BlockSpec((None, n_k, v.shape[2]), lambda b, i: (b, 0, 0)),
        ],
        out_specs=pl.BlockSpec((None, tq, n_kv * group * dv), lambda b, i: (b, i, 0)),
        out_shape=jax.ShapeDtypeStruct((bsz, n_q, n_kv * group * dv), BF16),
        scratch_shapes=head_scratch * n_kv,
        compiler_params=_params("parallel", "parallel"),
        name="attention",
    )(q, k, v)


def _kv_up_kernel(ckv_ref, kr_ref, w_ref, k_ref, v_ref):
    r = jnp.dot(ckv_ref[...], w_ref[...], preferred_element_type=F32)
    kr = kr_ref[...]
    for hd in range(MLA_HEADS):
        k_ref[:, hd * MLA_QPAD:hd * MLA_QPAD + MLA_NOPE] = r[:, hd * MLA_NOPE:(hd + 1) * MLA_NOPE].astype(BF16)
        k_ref[:, hd * MLA_QPAD + MLA_NOPE:(hd + 1) * MLA_QPAD] = kr
    v_ref[...] = r[:, MLA_HEADS * MLA_NOPE:].astype(BF16)


def _kv_up(ckv, kr, w_ukv, l):
    m_rows = ckv.shape[0]
    tm = _tile(m_rows, 512, 16)
    n_out = MLA_HEADS * (MLA_NOPE + MLA_V)
    return pl.pallas_call(
        _kv_up_kernel,
        grid=(m_rows // tm,),
        in_specs=[
            pl.BlockSpec((tm, MLA_KV_RANK), lambda i: (i, 0)),
            pl.BlockSpec((tm, 128), lambda i: (i, 0)),
            pl.BlockSpec((None, MLA_KV_RANK, n_out), lambda i: (l, 0, 0)),
        ],
        out_specs=[
            pl.BlockSpec((tm, MLA_HEADS * MLA_QPAD), lambda i: (i, 0)),
            pl.BlockSpec((tm, MLA_HEADS * MLA_V), lambda i: (i, 0)),
        ],
        out_shape=[
            jax.ShapeDtypeStruct((m_rows, MLA_HEADS * MLA_QPAD), BF16),
            jax.ShapeDtypeStruct((m_rows, MLA_HEADS * MLA_V), BF16),
        ],
        compiler_params=_params("parallel"),
        name="mla_kv_up",
    )(ckv, kr, w_ukv)


def _lru_kernel(*refs, reverse, combine, n_chunks):
    xp_ref, xm_ref, xn_ref, cw_ref, cb_ref, wax_ref, ba_ref, bx_ref, lam_ref, h0_ref = refs[:10]
    if combine:
        hb_ref, g_ref, y_ref, st_ref, a_s, u_s, h_s, carry = refs[10:]
    else:
        ho_ref, st_ref, a_s, u_s, h_s, carry = refs[10:]
    c = pl.program_id(1)
    cc = (n_chunks - 1 - c) if reverse else c
    tc = xm_ref.shape[0]

    xp = jnp.where(cc > 0, xp_ref[...], 0.0)
    xn = jnp.where(cc < n_chunks - 1, xn_ref[...], 0.0)
    xe = jnp.concatenate([xp, xm_ref[...], xn], axis=0)
    cw = cw_ref[...]
    xc = xe[7:7 + tc] * cw[0:1]
    for j in range(1, LRU_CONV_W):
        xc = xc + xe[7 + j:7 + j + tc] * cw[j:j + 1]
    xc = xc + cb_ref[...]

    lam = lam_ref[...]
    ls_c = LRU_C * (jnp.minimum(lam, 0.0) - jnp.log1p(jnp.exp(-jnp.abs(lam))))
    xcb = xc.astype(BF16)
    for n in range(LRU_BLOCKS):
        sl = slice(n * LRU_BLOCK_W, (n + 1) * LRU_BLOCK_W)
        gg = jnp.dot(xcb[:, sl], wax_ref[n], preferred_element_type=F32)
        r = jax.nn.sigmoid(gg[:, :LRU_BLOCK_W] + ba_ref[:, sl])
        i = jax.nn.sigmoid(gg[:, LRU_BLOCK_W:] + bx_ref[:, sl])
        log_a = r * ls_c[:, sl]
        a = jnp.exp(log_a)
        a_s[:, sl] = a
        u_s[:, sl] = jnp.sqrt(-jnp.tanh(log_a) * (a * a + 1.0)) * (i * xc[:, sl])

    @pl.when(c == 0)
    def _():
        carry[...] = h0_ref[...]

    def body(j, h):
        t = (tc - 1 - j) if reverse else j
        h = a_s[pl.ds(t, 1), :] * h + u_s[pl.ds(t, 1), :]
        h_s[pl.ds(t, 1), :] = h
        return h

    h = lax.fori_loop(0, tc, body, carry[...], unroll=8)
    carry[...] = h
    st_ref[...] = h
    if combine:
        y_ref[...] = ((h_s[...] + hb_ref[...]) * jax.nn.gelu(g_ref[...])).astype(BF16)
    else:
        ho_ref[...] = h_s[...]


def _lru_pass(x_r, conv_w, conv_b, wax, b_a, b_x, lam, h0, l, direction, hb=None, g_r=None):
    bsz, seq, w = x_r.shape
    combine = hb is not None
    reverse = direction == 1
    tc = _tile(seq, 256, 8)
    n_chunks = seq // tc
    t8 = tc // 8
    n8 = seq // 8
    d = direction

    def chunk(c):
        return (n_chunks - 1 - c) if reverse else c

    main = lambda b, c: (b, chunk(c), 0)
    in_specs = [
        pl.BlockSpec((None, 8, w), lambda b, c: (b, jnp.maximum(chunk(c) * t8 - 1, 0), 0)),
        pl.BlockSpec((None, tc, w), main),
        pl.BlockSpec((None, 8, w), lambda b, c: (b, jnp.minimum((chunk(c) + 1) * t8, n8 - 1), 0)),
        pl.BlockSpec((None, LRU_CONV_W, w), lambda b, c: (l, 0, 0)),
        pl.BlockSpec((None, 1, w), lambda b, c: (l, 0, 0)),
        pl.BlockSpec((None, None, LRU_BLOCKS, LRU_BLOCK_W, 2 * LRU_BLOCK_W), lambda b, c: (l, d, 0, 0, 0)),
        pl.BlockSpec((None, None, 1, w), lambda b, c: (l, d, 0, 0)),
        pl.BlockSpec((None, None, 1, w), lambda b, c: (l, d, 0, 0)),
        pl.BlockSpec((None, None, 1, w), lambda b, c: (l, d, 0, 0)),
        pl.BlockSpec((None, None, 1, w), lambda b, c: (b, d, 0, 0)),
    ]
    args = [x_r, x_r, x_r, conv_w, conv_b, wax, b_a, b_x, lam, h0]
    st_spec = pl.BlockSpec((None, 1, w), lambda b, c: (b, 0, 0))
    st_shape = jax.ShapeDtypeStruct((bsz, 1, w), F32)
    if combine:
        in_specs += [pl.BlockSpec((None, tc, w), main), pl.BlockSpec((None, tc, w), main)]
        args += [hb, g_r]
        out_specs = [pl.BlockSpec((None, tc, w), main), st_spec]
        out_shape = [jax.ShapeDtypeStruct((bsz, seq, w), BF16), st_shape]
    else:
        out_specs = [pl.BlockSpec((None, tc, w), main), st_spec]
        out_shape = [jax.ShapeDtypeStruct((bsz, seq, w), F32), st_shape]
    return pl.pallas_call(
        functools.partial(_lru_kernel, reverse=reverse, combine=combine, n_chunks=n_chunks),
        grid=(bsz, n_chunks),
        in_specs=in_specs,
        out_specs=out_specs,
        out_shape=out_shape,
        scratch_shapes=[pltpu.VMEM((tc, w), F32)] * 3 + [pltpu.VMEM((1, w), F32)],
        compiler_params=_params("parallel", "arbitrary"),
        name="rglru_bwd" if reverse else "rglru_fwd",
    )(*args)


def _mix_out_kernel(oa_ref, yr_ref, om_ref, w_ref, x_ref, mod_ref, gpost_ref, gffn_ref, xo_ref, h_ref):
    n_a = oa_ref.shape[1]
    n_r = yr_ref.shape[1]
    out = jnp.dot(oa_ref[...], w_ref[0:n_a, :], preferred_element_type=F32)
    out = out + jnp.dot(yr_ref[...], w_ref[n_a:n_a + n_r, :], preferred_element_type=F32)
    out = out + jnp.dot(om_ref[...], w_ref[n_a + n_r:, :], preferred_element_type=F32)
    m = mod_ref[...]
    xn = x_ref[...] + m[2:3] * (_rms(out) * gpost_ref[...])
    xo_ref[...] = xn
    h_ref[...] = (_rms(xn) * gffn_ref[...] * (1.0 + m[4:5]) + m[3:4]).astype(BF16)


def _mix_out(o_a, y_r, o_m, w_out, x, mod, g_post, g_ffn, l):
    m_rows, d = x.shape
    n_mod = mod.shape[1]
    tm = _tile(m_rows // n_mod, 512, 16)
    per = (m_rows // n_mod) // tm
    row = lambda i: (i, 0)
    d_mix = w_out.shape[1]
    return pl.pallas_call(
        _mix_out_kernel,
        grid=(m_rows // tm,),
        in_specs=[
            pl.BlockSpec((tm, o_a.shape[1]), row),
            pl.BlockSpec((tm, y_r.shape[1]), row),
            pl.BlockSpec((tm, o_m.shape[1]), row),
            pl.BlockSpec((None, d_mix, d), lambda i: (l, 0, 0)),
            pl.BlockSpec((tm, d), row),
            pl.BlockSpec((None, None, 8, d), lambda i: (l, i // per, 0, 0)),
            pl.BlockSpec((None, 1, d), lambda i: (l, 0, 0)),
            pl.BlockSpec((None, 1, d), lambda i: (l, 0, 0)),
        ],
        out_specs=[pl.BlockSpec((tm, d), row), pl.BlockSpec((tm, d), row)],
        out_shape=[jax.ShapeDtypeStruct((m_rows, d), F32), jax.ShapeDtypeStruct((m_rows, d), BF16)],
        compiler_params=_params("parallel"),
        name="mix_out",
    )(o_a, y_r, o_m, w_out, x, mod, g_post, g_ffn)


HALO = 16
N_CHUNKS = 4


def _ffn_up_kernel(hp_ref, hm_ref, hn_ref, wg_ref, wv_ref, cwg_ref, cwv_ref, cbg_ref, cbv_ref, o_ref,
                   he_ref, ug0_ref, uv0_ref, ug1_ref, uv1_ref, *, seq, seg, nj):
    i = pl.program_id(0)
    j = pl.program_id(1)
    tm = hm_ref.shape[0]
    n_seg = tm // seg
    rows = he_ref.shape[0]
    bufs = ((ug0_ref, uv0_ref), (ug1_ref, uv1_ref))

    @pl.when(j == 0)
    def _():
        for s in range(n_seg):
            o = HALO + s * (seg + HALO)
            he_ref[o:o + seg, :] = hm_ref[s * seg:(s + 1) * seg, :]
            if s > 0:
                he_ref[o - HALO:o, :] = jnp.zeros((HALO, he_ref.shape[1]), BF16)
        if n_seg == 1:
            he_ref[0:HALO, :] = jnp.where((i * tm) % seq != 0, hp_ref[...], jnp.zeros_like(hp_ref))
            he_ref[rows - HALO:, :] = jnp.where(((i + 1) * tm) % seq != 0, hn_ref[...], jnp.zeros_like(hn_ref))
        else:
            he_ref[0:HALO, :] = jnp.zeros((HALO, he_ref.shape[1]), BF16)
            he_ref[rows - HALO:, :] = jnp.zeros((HALO, he_ref.shape[1]), BF16)

    @pl.when((i == 0) & (j == 0))
    def _():
        ug1_ref[...] = jnp.zeros_like(ug1_ref)
        uv1_ref[...] = jnp.zeros_like(uv1_ref)

    units = rows // HALO
    cuts = [HALO * ((units * c) // N_CHUNKS) for c in range(N_CHUNKS + 1)]
    gc = min(seg, max(tm // N_CHUNKS, 8))
    gate_chunks = [(HALO + s * (seg + HALO) + c0, s * seg + c0) for s in range(n_seg) for c0 in range(0, seg, gc)]

    def project(slot, c):
        he = he_ref[cuts[c]:cuts[c + 1], :]
        bufs[slot][0][cuts[c]:cuts[c + 1], :] = jnp.dot(he, wg_ref[...], preferred_element_type=F32)
        bufs[slot][1][cuts[c]:cuts[c + 1], :] = jnp.dot(he, wv_ref[...], preferred_element_type=F32)

    rs = min(gc, 64)

    def conv(u_ref, cw, cb, r0, c0):
        u = u_ref[r0 - 8:r0 + rs + 8, c0:c0 + 128]
        u_prev = pltpu.roll(u, 1, 0)
        u_next = pltpu.roll(u, rs + 15, 0)
        return u_prev[8:8 + rs] * cw[0:1] + u[8:8 + rs] * cw[1:2] + u_next[8:8 + rs] * cw[2:3] + cb

    def gate(slot, off, out_row):
        for c0 in range(0, o_ref.shape[1], 128):
            cs = slice(c0, c0 + 128)
            cwg, cbg, cwv, cbv = cwg_ref[:, cs], cbg_ref[:, cs], cwv_ref[:, cs], cbv_ref[:, cs]
            for r in range(0, gc, rs):
                g = conv(bufs[slot][0], cwg, cbg, off + r, c0)
                v = conv(bufs[slot][1], cwv, cbv, off + r, c0)
                o_ref[out_row + r:out_row + r + rs, cs] = (g * jax.nn.sigmoid(g) * v).astype(BF16)

    for parity in (0, 1):
        @pl.when((j < nj) & (j % 2 == parity))
        def _(parity=parity):
            n_g = len(gate_chunks)
            for c in range(N_CHUNKS):
                project(parity, c)
                for off, out_row in gate_chunks[(c * n_g) // N_CHUNKS:((c + 1) * n_g) // N_CHUNKS]:
                    gate(1 - parity, off, out_row)

    @pl.when(j == nj)
    def _():
        for off, out_row in gate_chunks:
            gate((nj - 1) % 2, off, out_row)


def _ffn_up(h, w_up, conv_w, conv_b, l, seq):
    m_rows, d = h.shape
    d_ff = w_up.shape[2] // 2
    tm = _tile(m_rows, 1024, 16)
    if tm % seq != 0 and seq % tm != 0:
        tm = _tile(seq, 1024, 16)
    seg = min(tm, seq)
    tn = _tile(d_ff, 512)
    nj = d_ff // tn
    th = tm // HALO
    nh = m_rows // HALO
    he_rows = (tm // seg) * (seg + HALO) + HALO
    cur = lambda j: jnp.minimum(j, nj - 1)
    prev = lambda j: jnp.maximum(j - 1, 0)
    return pl.pallas_call(
        functools.partial(_ffn_up_kernel, seq=seq, seg=seg, nj=nj),
        grid=(m_rows // tm, nj + 1),
        in_specs=[
            pl.BlockSpec((HALO, d), lambda i, j: (jnp.maximum(i * th - 1, 0), 0)),
            pl.BlockSpec((tm, d), lambda i, j: (i, 0)),
            pl.BlockSpec((HALO, d), lambda i, j: (jnp.minimum((i + 1) * th, nh - 1), 0)),
            pl.BlockSpec((None, d, tn), lambda i, j: (l, 0, cur(j))),
            pl.BlockSpec((None, d, tn), lambda i, j: (l, 0, cur(j) + nj)),
            pl.BlockSpec((None, 3, tn), lambda i, j: (l, 0, prev(j))),
            pl.BlockSpec((None, 3, tn), lambda i, j: (l, 0, prev(j) + nj)),
            pl.BlockSpec((None, 1, tn), lambda i, j: (l, 0, prev(j))),
            pl.BlockSpec((None, 1, tn), lambda i, j: (l, 0, prev(j) + nj)),
        ],
        out_specs=pl.BlockSpec((tm, tn), lambda i, j: (i, prev(j))),
        out_shape=jax.ShapeDtypeStruct((m_rows, d_ff), BF16),
        scratch_shapes=[pltpu.VMEM((he_rows, d), BF16)] + [pltpu.VMEM((he_rows, tn), F32)] * 4,
        compiler_params=_params("arbitrary", "arbitrary"),
        name="ffn_up",
    )(h, h, h, w_up, w_up, conv_w, conv_w, conv_b, conv_b)


def _ffn_down_kernel(*refs, has_next, n_tiles):
    a_ref, w_ref, x_ref, mod_ref, gpost_ref = refs[:5]
    if has_next:
        modn_ref, gnext_ref, xo_ref, hn_ref, acc0_ref, acc1_ref = refs[5:]
    else:
        xo_ref, acc0_ref, acc1_ref = refs[5:]
    accs = (acc0_ref, acc1_ref)
    i = pl.program_id(0)
    k = pl.program_id(1)

    @pl.when((i == 0) & (k == 0))
    def _():
        acc1_ref[...] = jnp.zeros_like(acc1_ref)

    def finish(slot):
        m = mod_ref[...]
        xn = x_ref[...] + m[5:6] * (_rms(accs[slot][...]) * gpost_ref[...])
        xo_ref[...] = xn
        if has_next:
            mn = modn_ref[...]
            hn_ref[...] = (_rms(xn) * gnext_ref[...] * (1.0 + mn[1:2]) + mn[0:1]).astype(BF16)

    for parity in (0, 1):
        @pl.when((i < n_tiles) & (i % 2 == parity) & (k == 0))
        def _(parity=parity):
            accs[parity][...] = jnp.dot(a_ref[...], w_ref[...], preferred_element_type=F32)
            finish(1 - parity)

        @pl.when((i < n_tiles) & (i % 2 == parity) & (k > 0))
        def _(parity=parity):
            accs[parity][...] += jnp.dot(a_ref[...], w_ref[...], preferred_element_type=F32)

    @pl.when((i == n_tiles) & (k == 0))
    def _():
        finish((n_tiles - 1) % 2)


def _ffn_down(act, w_down, x, mod, g_post, g_pre_mix, l, has_next):
    m_rows, d = x.shape
    d_ff = act.shape[1]
    n_mod = mod.shape[1]
    tm = _tile(m_rows // n_mod, 512, 16)
    per = (m_rows // n_mod) // tm
    tk = _tile(d_ff, 1408)
    n_tiles = m_rows // tm
    nk = d_ff // tk
    cur = lambda i: jnp.minimum(i, n_tiles - 1)
    prev = lambda i: jnp.maximum(i - 1, 0)
    kk = lambda i, k: jnp.where(i == n_tiles, nk - 1, k)
    row = lambda i, k: (prev(i), 0)
    in_specs = [
        pl.BlockSpec((tm, tk), lambda i, k: (cur(i), kk(i, k))),
        pl.BlockSpec((None, tk, d), lambda i, k: (l, kk(i, k), 0)),
        pl.BlockSpec((tm, d), row),
        pl.BlockSpec((None, None, 8, d), lambda i, k: (l, prev(i) // per, 0, 0)),
        pl.BlockSpec((None, 1, d), lambda i, k: (l, 0, 0)),
    ]
    args = [act, w_down, x, mod, g_post]
    out_specs = [pl.BlockSpec((tm, d), row)]
    out_shape = [jax.ShapeDtypeStruct((m_rows, d), F32)]
    if has_next:
        in_specs += [
            pl.BlockSpec((None, None, 8, d), lambda i, k: (l + 1, prev(i) // per, 0, 0)),
            pl.BlockSpec((None, 1, d), lambda i, k: (l + 1, 0, 0)),
        ]
        args += [mod, g_pre_mix]
        out_specs.append(pl.BlockSpec((tm, d), row))
        out_shape.append(jax.ShapeDtypeStruct((m_rows, d), BF16))
    res = pl.pallas_call(
        functools.partial(_ffn_down_kernel, has_next=has_next, n_tiles=n_tiles),
        grid=(n_tiles + 1, nk),
        in_specs=in_specs,
        out_specs=out_specs,
        out_shape=out_shape,
        scratch_shapes=[pltpu.VMEM((tm, d), F32)] * 2,
        compiler_params=_params("arbitrary", "arbitrary"),
        name="ffn_down",
    )(*args)
    return (res[0], res[1]) if has_next else (res[0], None)


def _rope_tables(n_tok):
    rows = n_tok // GRID_W
    row = jnp.repeat(jnp.arange(rows, dtype=jnp.int32), GRID_W)
    col = jnp.tile(jnp.arange(GRID_W, dtype=jnp.int32), rows)

    def tabs(rot_dim):
        quarter = rot_dim // 4
        inv_freq = ROPE_THETA ** (-jnp.arange(quarter, dtype=F32) / quarter)
        cs, ss = [], []
        for p in (row, col):
            ang = p.astype(F32)[:, None] * inv_freq[None, :]
            cs += [jnp.cos(ang), jnp.cos(ang)]
            ss += [-jnp.sin(ang), jnp.sin(ang)]
        return jnp.concatenate(cs, axis=-1), jnp.concatenate(ss, axis=-1)

    c128, s128 = tabs(HEAD_DIM)
    c64, s64 = tabs(MLA_ROPE)
    pad = ((0, 0), (0, 128 - MLA_ROPE))
    return c128, s128, jnp.pad(c64, pad), jnp.pad(s64, pad)


def _run_path(x, bsz, seq, mod, wts, ctx):
    n_l = wts["w_att"].shape[0]
    d = x.shape[1]
    latent = ctx is not None
    rope_tabs = _rope_tables(seq) if latent else None
    outs = {"k": [], "v": [], "ckv": [], "kr": [], "st": []}
    h = _norm_mod(x, mod, wts["g_pre_mix"], 0)
    for l in range(n_l):
        proj = _attn_proj(h, wts["w_att"], wts["g_q"], wts["g_k"], wts["g_kv"], l, seq, rope_tabs, not latent)
        q_a, k_a, v_a, q_m, ckv, k_r = proj[:6]
        x_r, g_r = _lru_proj(h, wts["w_lru"], l)

        k_all = k_a.reshape(bsz, seq, ATTN_KV_COLS)
        v_all = v_a.reshape(bsz, seq, ATTN_KV_COLS)
        ckv_all = ckv.reshape(bsz, seq, MLA_KV_RANK)
        kr_all = k_r.reshape(bsz, seq, 128)
        if latent:
            k_all = jnp.concatenate([ctx["k"][:, l], k_all], axis=1)
            v_all = jnp.concatenate([ctx["v"][:, l], v_all], axis=1)
            ckv_all = jnp.concatenate([ctx["ckv"][:, l], ckv_all], axis=1)
            kr_all = jnp.concatenate([ctx["kr"][:, l], kr_all], axis=1)
            h0 = ctx["st"]
            h0_l = l
        else:
            kf, vf, ckvf, krf = proj[6:]
            outs["k"].append(kf.reshape(bsz, seq, ATTN_KV_HEADS, HEAD_DIM))
            outs["v"].append(vf.reshape(bsz, seq, ATTN_KV_HEADS, HEAD_DIM))
            outs["ckv"].append(ckvf.reshape(bsz, seq, MLA_KV_RANK))
            outs["kr"].append(krf.reshape(bsz, seq, MLA_ROPE))
            h0 = jnp.zeros((bsz, 1, 2, 1, LRU_WIDTH), F32)
            h0_l = 0
        n_k = k_all.shape[1]

        o_a = _attention(q_a.reshape(bsz, seq, ATTN_Q_COLS), k_all, v_all,
                         ATTN_KV_HEADS, ATTN_HEADS // ATTN_KV_HEADS, HEAD_DIM, HEAD_DIM, HEAD_DIM ** -0.5)
        k_m, v_m = _kv_up(ckv_all.reshape(bsz * n_k, MLA_KV_RANK), kr_all.reshape(bsz * n_k, 128), wts["w_ukv"], l)
        o_m = _attention(q_m.reshape(bsz, seq, MLA_HEADS * MLA_QPAD),
                         k_m.reshape(bsz, n_k, MLA_HEADS * MLA_QPAD), v_m.reshape(bsz, n_k, MLA_HEADS * MLA_V),
                         MLA_HEADS, 1, MLA_QPAD, MLA_V, (MLA_NOPE + MLA_ROPE) ** -0.5)

        x_r3 = x_r.reshape(bsz, seq, LRU_WIDTH)
        lru_args = (x_r3, wts["lru_conv_w"], wts["lru_conv_b"], wts["lru_wax"], wts["lru_b_a"], wts["lru_b_x"],
                    wts["lru_lambda"], h0[:, h0_l])
        hb, st_b = _lru_pass(*lru_args, l, 1)
        y_r, st_f = _lru_pass(*lru_args, l, 0, hb=hb, g_r=g_r.reshape(bsz, seq, LRU_WIDTH))
        if not latent:
            outs["st"].append(jnp.concatenate([st_f, st_b], axis=1))

        x, h2 = _mix_out(o_a.reshape(bsz * seq, -1), y_r.reshape(bsz * seq, LRU_WIDTH), o_m.reshape(bsz * seq, -1),
                         wts["w_out"], x, mod, wts["g_post_mix"], wts["g_pre_ffn"], l)
        act = _ffn_up(h2, wts["ffn_w_up"], wts["ffn_conv_w"], wts["ffn_conv_b"], l, seq)
        x, h = _ffn_down(act, wts["ffn_w_down"], x, mod, wts["g_post_ffn"], wts["g_pre_mix"], l, l + 1 < n_l)
    return x.reshape(bsz, seq, d), outs


def kernel(x_prompt, x_sample, cache_attn_k, cache_attn_v, cache_mla_ckv, cache_mla_krope, state_lru, c, c_ctx, w_mod, b_mod, g_pre_mix, g_post_mix, g_pre_ffn, g_post_ffn, w_in, g_q, g_k, lru_conv_w, lru_conv_b, lru_w_a, lru_b_a, lru_w_x, lru_b_x, lru_lambda, g_kv, w_uk, w_uv, w_out, ffn_w_up, ffn_conv_w, ffn_conv_b, ffn_w_down):
    n_l, d, _ = w_in.shape
    bsz, seq, _ = x_prompt.shape
    dbsz, dseq, _ = x_sample.shape
    assert dbsz <= 7 and w_in.shape[2] == D_IN

    cond = jnp.zeros((8, d), F32).at[0].set(c_ctx).at[1:1 + dbsz].set(c)
    mod_all = _mod_all(cond, w_mod, b_mod).reshape(n_l, 8, N_MOD, d)
    mod_all = jnp.pad(mod_all, ((0, 0), (0, 0), (0, 8 - N_MOD), (0, 0)))
    mod_p = mod_all[:, 0:1]
    mod_s = mod_all[:, 1:1 + dbsz]

    qm = w_in[:, :, OFF_QM:OFF_CKV].reshape(n_l, d, MLA_HEADS, MLA_NOPE + MLA_ROPE)
    qm = jnp.pad(qm, ((0, 0), (0, 0), (0, 0), (0, MLA_QPAD - MLA_NOPE - MLA_ROPE))).reshape(n_l, d, MLA_HEADS * MLA_QPAD)
    w_att = jnp.concatenate(
        [w_in[:, :, OFF_QA:OFF_XR], qm, w_in[:, :, OFF_CKV:OFF_KR],
         jnp.pad(w_in[:, :, OFF_KR:], ((0, 0), (0, 0), (0, 128 - MLA_ROPE)))], axis=-1).astype(BF16)
    vec = lambda a: a.reshape(a.shape[:-1] + (1, a.shape[-1]))
    wts = {
        "w_att": w_att,
        "w_lru": w_in[:, :, OFF_XR:OFF_QM].astype(BF16),
        "g_q": vec(g_q), "g_k": vec(g_k), "g_kv": vec(g_kv),
        "g_pre_mix": vec(g_pre_mix), "g_post_mix": vec(g_post_mix),
        "g_pre_ffn": vec(g_pre_ffn), "g_post_ffn": vec(g_post_ffn),
        "lru_conv_w": lru_conv_w, "lru_conv_b": vec(lru_conv_b),
        "lru_wax": jnp.concatenate([lru_w_a, lru_w_x], axis=-1).astype(BF16),
        "lru_b_a": vec(lru_b_a), "lru_b_x": vec(lru_b_x), "lru_lambda": vec(lru_lambda),
        "w_ukv": jnp.concatenate([w_uk.reshape(n_l, MLA_KV_RANK, -1), w_uv.reshape(n_l, MLA_KV_RANK, -1)],
                                 axis=-1).astype(BF16),
        "w_out": w_out.astype(BF16),
        "ffn_w_up": ffn_w_up.astype(BF16),
        "ffn_conv_w": ffn_conv_w, "ffn_conv_b": vec(ffn_conv_b),
        "ffn_w_down": ffn_w_down.astype(BF16),
    }

    y_prompt, outs = _run_path(x_prompt.reshape(bsz * seq, d), bsz, seq, mod_p, wts, None)

    past = cache_attn_k.shape[2]
    ctx = {
        "k": cache_attn_k.reshape(dbsz, n_l, past, ATTN_KV_COLS).astype(BF16),
        "v": cache_attn_v.reshape(dbsz, n_l, past, ATTN_KV_COLS).astype(BF16),
        "ckv": cache_mla_ckv.astype(BF16),
        "kr": jnp.pad(cache_mla_krope, ((0, 0), (0, 0), (0, 0), (0, 128 - MLA_ROPE))).astype(BF16),
        "st": state_lru.reshape(dbsz, n_l, 2, 1, LRU_WIDTH),
    }
    y_sample, _ = _run_path(x_sample.reshape(dbsz * dseq, d), dbsz, dseq, mod_s, wts, ctx)

    return (y_prompt, y_sample,
            jnp.stack(outs["k"], axis=1), jnp.stack(outs["v"], axis=1),
            jnp.stack(outs["ckv"], axis=1), jnp.stack(outs["kr"], axis=1),
            jnp.stack(outs["st"], axis=1))
```
